```python
import jax, jax.numpy as jnp
from jax import lax
import numpy as np

D_MODEL = 2048
BATCH = 4
SEQ = 2048
DEPTH = 1
DEC_BATCH = 128
DEC_SEQ = 8
PAST_LEN = 16384
PAGE_SIZE = 128

MIX_WIDTH = D_MODEL
C_CONV = MIX_WIDTH // 2
CONV_WIDTH = 31
RWKV_HEAD_DIM = 64
RWKV_WIDTH = MIX_WIDTH - C_CONV
RWKV_HEADS = RWKV_WIDTH // RWKV_HEAD_DIM
DECAY_LORA = 96
A_LORA = 96
GATE_LORA = 256
N_RWKV_COLS = 3 * RWKV_WIDTH + DECAY_LORA + A_LORA + GATE_LORA
N_IN_COLS = 2 * C_CONV + N_RWKV_COLS
N_MEM = 256
XA_HEADS = 4
XA_HEAD_DIM = D_MODEL // XA_HEADS
D_FF = 11 * D_MODEL // 4
FFN_CONV_WIDTH = 3
RMS_EPS = 1e-6
LN_EPS = 1e-5
GN_EPS = 64e-5

kernel_name = 'hymba_conformer_rwkv7_convffn_xattn_step'


def rms_norm(x, g):
    xf = x.astype(jnp.float32)
    y = xf * lax.rsqrt(jnp.mean(xf * xf, axis=-1, keepdims=True) + RMS_EPS)
    return (y * g.astype(jnp.float32)).astype(x.dtype)


def layer_norm(x, g, b):
    xf = x.astype(jnp.float32)
    mu = jnp.mean(xf, axis=-1, keepdims=True)
    var = jnp.mean(jnp.square(xf - mu), axis=-1, keepdims=True)
    y = (xf - mu) * lax.rsqrt(var + LN_EPS) * g.astype(jnp.float32) + b.astype(jnp.float32)
    return y.astype(x.dtype)


def causal_dwconv(u, past, w, bias):
    k = w.shape[0]
    t = u.shape[1]
    full = jnp.concatenate([past.astype(u.dtype), u], axis=1)
    wf = w.astype(jnp.float32)
    acc = bias.astype(jnp.float32)
    for j in range(k):
        acc = acc + full[:, j:j + t].astype(jnp.float32) * wf[j]
    return acc.astype(u.dtype), full[:, t:]


def wkv_step(S, inp):
    r, w, k, v, a, b = inp
    sa = jnp.einsum('bhij,bhj->bhi', S, a)
    S = S * w[:, :, None, :] + sa[..., None] * b[:, :, None, :] + v[..., None] * k[:, :, None, :]
    y = jnp.einsum('bhij,bhj->bhi', S, r)
    return S, y


def rwkv7_mix(xs, wkv, p):
    f32 = jnp.float32
    b_, t_ = xs.shape[:2]
    W = RWKV_WIDTH
    heads = lambda z: z.reshape(b_, t_, RWKV_HEADS, RWKV_HEAD_DIM)
    r = xs[..., :W].astype(f32)
    k = xs[..., W:2 * W].astype(f32)
    v = xs[..., 2 * W:3 * W].astype(f32)
    o = 3 * W
    wd = xs[..., o:o + DECAY_LORA]
    ad = xs[..., o + DECAY_LORA:o + DECAY_LORA + A_LORA]
    gd = xs[..., o + DECAY_LORA + A_LORA:]
    w_log = -jax.nn.softplus(-(p['w0'] + jnp.tanh(wd) @ p['w_lora']).astype(f32)) - 0.5
    decay = jnp.exp(-jnp.exp(w_log))
    a = jax.nn.sigmoid((p['a0'] + ad @ p['a_lora']).astype(f32))
    g = (jax.nn.sigmoid(gd) @ p['g_lora']).astype(f32)
    kk = heads(k * p['k_k'].astype(f32))
    kk = kk / jnp.maximum(jnp.sqrt(jnp.sum(kk * kk, axis=-1, keepdims=True)), 1e-12)
    k = k * (1.0 + (a - 1.0) * p['k_a'].astype(f32))
    rh, kh, vh, ah = heads(r), heads(k), heads(v), heads(a)
    seq = tuple(jnp.moveaxis(z, 1, 0) for z in (rh, heads(decay), kh, vh, -kk, kk * ah))
    S_new, ys = lax.scan(wkv_step, wkv.astype(f32), seq)
    y = jnp.moveaxis(ys, 0, 1)
    mu = jnp.mean(y, axis=-1, keepdims=True)
    var = jnp.mean(jnp.square(y - mu), axis=-1, keepdims=True)
    yn = ((y - mu) * lax.rsqrt(var + GN_EPS)).reshape(b_, t_, W)
    yn = yn * p['ln_x_g'].astype(f32) + p['ln_x_b'].astype(f32)
    bonus = jnp.sum(rh * kh * heads(jnp.broadcast_to(p['r_k'].astype(f32), r.shape)), axis=-1, keepdims=True) * vh
    out = (yn + bonus.reshape(b_, t_, W)) * g
    return out.astype(xs.dtype), S_new


def memory_kv(mem, p):
    b_ = mem.shape[0]
    m = rms_norm(mem, p['norm_mem'])
    k = (m @ p['w_k']).reshape(b_, N_MEM, XA_HEADS, XA_HEAD_DIM)
    v = (m @ p['w_v']).reshape(b_, N_MEM, XA_HEADS, XA_HEAD_DIM)
    return k, v


def trunk_layer(x, mem_k, mem_v, conv_buf, shift, wkv, ffn_buf, p):
    b_, t_ = x.shape[:2]
    h = rms_norm(x, p['norm_mix_pre'])
    proj = h @ p['w_in']
    glu = proj[..., :C_CONV] * jax.nn.sigmoid(proj[..., C_CONV:2 * C_CONV])
    cv, new_conv = causal_dwconv(glu, conv_buf, p['conv_dw'], p['conv_dw_b'])
    cv = jax.nn.silu(layer_norm(cv, p['conv_ln_g'], p['conv_ln_b']))
    pr = proj[..., 2 * C_CONV:]
    prev = jnp.concatenate([shift[:, None].astype(pr.dtype), pr[:, :-1]], axis=1)
    xs = pr + (prev - pr) * p['rwkv_mu']
    new_shift = pr[:, -1]
    rw, new_wkv = rwkv7_mix(xs, wkv, p)
    mix = jnp.concatenate([cv, rw.astype(cv.dtype)], axis=-1) @ p['w_out']
    x = x + rms_norm(mix, p['norm_mix_post'])
    h = rms_norm(x, p['norm_xa_pre'])
    q = (h @ p['w_q']).reshape(b_, t_, XA_HEADS, XA_HEAD_DIM)
    s = jnp.einsum('bthd,bmhd->bhtm', q, mem_k.astype(q.dtype)).astype(jnp.float32) * (XA_HEAD_DIM ** -0.5)
    attn = jax.nn.softmax(s, axis=-1).astype(x.dtype)
    o = jnp.einsum('bhtm,bmhd->bthd', attn, mem_v.astype(x.dtype)).reshape(b_, t_, XA_HEADS * XA_HEAD_DIM)
    x = x + rms_norm(o @ p['w_o'], p['norm_xa_post'])
    h = rms_norm(x, p['norm_ffn_pre'])
    up = h @ p['w_up']
    uc, new_ffn = causal_dwconv(up, ffn_buf, p['ffn_dw'], p['ffn_dw_b'])
    act = jax.nn.silu(uc[..., :D_FF]) * uc[..., D_FF:]
    x = x + rms_norm(act @ p['w_down'], p['norm_ffn_post'])
    return x, new_conv, new_shift, new_wkv, new_ffn


def setup_inputs(seed: int = 0) -> dict:
    key = jax.random.key(seed)
    ks = iter(jax.random.split(key, 64))
    nrm = lambda shape, s=1.0: jax.random.normal(next(ks), shape, jnp.float32) * s
    gain = lambda n: 1.0 + nrm((DEPTH, n), 0.05)
    L = DEPTH
    return {
        'x_prompt': nrm((BATCH, SEQ, D_MODEL)),
        'x_sample': nrm((DEC_BATCH, DEC_SEQ, D_MODEL)),
        'cache_mem_k': nrm((L, DEC_BATCH, N_MEM, XA_HEADS, XA_HEAD_DIM)),
        'cache_mem_v': nrm((L, DEC_BATCH, N_MEM, XA_HEADS, XA_HEAD_DIM)),
        'state_conv': nrm((L, DEC_BATCH, CONV_WIDTH - 1, C_CONV), 0.5),
        'state_shift': nrm((L, DEC_BATCH, N_RWKV_COLS)),
        'state_wkv': nrm((L, DEC_BATCH, RWKV_HEADS, RWKV_HEAD_DIM, RWKV_HEAD_DIM), 0.1),
        'state_ffn': nrm((L, DEC_BATCH, FFN_CONV_WIDTH - 1, 2 * D_FF)),
        'mem_prompt': nrm((BATCH, N_MEM, D_MODEL)),
        'norm_mix_pre': gain(D_MODEL),
        'w_in': nrm((L, D_MODEL, N_IN_COLS), D_MODEL ** -0.5),
        'conv_dw': nrm((L, CONV_WIDTH, C_CONV), CONV_WIDTH ** -0.5),
        'conv_dw_b': nrm((L, C_CONV), 0.02),
        'conv_ln_g': gain(C_CONV),
        'conv_ln_b': nrm((L, C_CONV), 0.02),
        'rwkv_mu': jax.random.uniform(next(ks), (L, N_RWKV_COLS), jnp.float32),
        'w0': jax.random.uniform(next(ks), (L, RWKV_WIDTH), jnp.float32, -6.0, 1.0),
        'w_lora': nrm((L, DECAY_LORA, RWKV_WIDTH), 0.1 * DECAY_LORA ** -0.5),
        'a0': nrm((L, RWKV_WIDTH), 0.1),
        'a_lora': nrm((L, A_LORA, RWKV_WIDTH), 0.1 * A_LORA ** -0.5),
        'g_lora': nrm((L, GATE_LORA, RWKV_WIDTH), GATE_LORA ** -0.5),
        'k_k': 0.85 + nrm((L, RWKV_WIDTH), 0.05),
        'k_a': 1.0 + nrm((L, RWKV_WIDTH), 0.05),
        'r_k': nrm((L, RWKV_WIDTH), 0.1),
        'ln_x_g': gain(RWKV_WIDTH),
        'ln_x_b': nrm((L, RWKV_WIDTH), 0.02),
        'w_out': nrm((L, MIX_WIDTH, D_MODEL), MIX_WIDTH ** -0.5),
        'norm_mix_post': gain(D_MODEL),
        'norm_xa_pre': gain(D_MODEL),
        'norm_mem': gain(D_MODEL),
        'w_q': nrm((L, D_MODEL, XA_HEADS * XA_HEAD_DIM), D_MODEL ** -0.5),
        'w_k': nrm((L, D_MODEL, XA_HEADS * XA_HEAD_DIM), D_MODEL ** -0.5),
        'w_v': nrm((L, D_MODEL, XA_HEADS * XA_HEAD_DIM), D_MODEL ** -0.5),
        'w_o': nrm((L, XA_HEADS * XA_HEAD_DIM, D_MODEL), (XA_HEADS * XA_HEAD_DIM) ** -0.5),
        'norm_xa_post': gain(D_MODEL),
        'norm_ffn_pre': gain(D_MODEL),
        'w_up': nrm((L, D_MODEL, 2 * D_FF), D_MODEL ** -0.5),
        'ffn_dw': nrm((L, FFN_CONV_WIDTH, 2 * D_FF), FFN_CONV_WIDTH ** -0.5),
        'ffn_dw_b': nrm((L, 2 * D_FF), 0.02),
        'w_down': nrm((L, D_FF, D_MODEL), D_FF ** -0.5),
        'norm_ffn_post': gain(D_MODEL),
    }


def reference(x_prompt, x_sample, cache_mem_k, cache_mem_v, state_conv, state_shift, state_wkv, state_ffn,
              mem_prompt, norm_mix_pre, w_in, conv_dw, conv_dw_b, conv_ln_g, conv_ln_b, rwkv_mu, w0, w_lora,
              a0, a_lora, g_lora, k_k, k_a, r_k, ln_x_g, ln_x_b, w_out, norm_mix_post, norm_xa_pre, norm_mem,
              w_q, w_k, w_v, w_o, norm_xa_post, norm_ffn_pre, w_up, ffn_dw, ffn_dw_b, w_down, norm_ffn_post):
    bp = x_prompt.shape[0]
    dt = x_prompt.dtype
    yp, ys = x_prompt, x_sample
    conv_p, conv_s, shift_p, shift_s, wkv_p, wkv_s, ffn_p, ffn_s, memk_p, memv_p = ([] for _ in range(10))
    for l in range(DEPTH):
        p = {'norm_mix_pre': norm_mix_pre[l], 'w_in': w_in[l], 'conv_dw': conv_dw[l], 'conv_dw_b': conv_dw_b[l],
             'conv_ln_g': conv_ln_g[l], 'conv_ln_b': conv_ln_b[l], 'rwkv_mu': rwkv_mu[l], 'w0': w0[l],
             'w_lora': w_lora[l], 'a0': a0[l], 'a_lora': a_lora[l], 'g_lora': g_lora[l], 'k_k': k_k[l],
             'k_a': k_a[l], 'r_k': r_k[l], 'ln_x_g': ln_x_g[l], 'ln_x_b': ln_x_b[l], 'w_out': w_out[l],
             'norm_mix_post': norm_mix_post[l], 'norm_xa_pre': norm_xa_pre[l], 'norm_mem': norm_mem[l],
             'w_q': w_q[l], 'w_k': w_k[l], 'w_v': w_v[l], 'w_o': w_o[l], 'norm_xa_post': norm_xa_post[l],
             'norm_ffn_pre': norm_ffn_pre[l], 'w_up': w_up[l], 'ffn_dw': ffn_dw[l], 'ffn_dw_b': ffn_dw_b[l],
             'w_down': w_down[l], 'norm_ffn_post': norm_ffn_post[l]}
        mk, mv = memory_kv(mem_prompt, p)
        yp, c1, s1, w1, f1 = trunk_layer(
            yp, mk, mv,
            jnp.zeros((bp, CONV_WIDTH - 1, C_CONV), dt),
            jnp.zeros((bp, N_RWKV_COLS), dt),
            jnp.zeros((bp, RWKV_HEADS, RWKV_HEAD_DIM, RWKV_HEAD_DIM), jnp.float32),
            jnp.zeros((bp, FFN_CONV_WIDTH - 1, 2 * D_FF), dt), p)
        ys, c2, s2, w2, f2 = trunk_layer(ys, cache_mem_k[l], cache_mem_v[l], state_conv[l], state_shift[l],
                                         state_wkv[l], state_ffn[l], p)
        conv_p.append(c1); conv_s.append(c2); shift_p.append(s1); shift_s.append(s2)
        wkv_p.append(w1); wkv_s.append(w2); ffn_p.append(f1); ffn_s.append(f2)
        memk_p.append(mk); memv_p.append(mv)
    return (yp, ys, jnp.stack(conv_p), jnp.stack(conv_s), jnp.stack(shift_p), jnp.stack(shift_s),
            jnp.stack(wkv_p), jnp.stack(wkv_s), jnp.stack(ffn_p), jnp.stack(ffn_s),
            jnp.stack(memk_p), jnp.stack(memv_p))
```

```python
import functools
import math

import jax
import jax.numpy as jnp
from jax import lax
from jax.experimental import pallas as pl
from jax.experimental.pallas import tpu as pltpu

F32 = jnp.float32
BF16 = jnp.bfloat16
RMS_EPS = 1e-6
LN_EPS = 1e-5
GN_EPS = 64e-5
LANES = 128
ROW_CHUNK = 16
HIGHEST = lax.Precision.HIGHEST
DECAY_SCALE = math.exp(-0.5)
_NT = (((1,), (1,)), ((), ()))
_TN = (((0,), (0,)), ((), ()))


_TILES = dict(
    tm_a=1024, tn_a=512,
    tm_e=512, tk_e=512,
    tt_conv=256, tt_prep=256,
    chunk=64, pp=4,
    tq=256,
    tm_f=512, tn_f=512,
)


def _cparams(n_grid, vmem_mib=48):
    return pltpu.CompilerParams(dimension_semantics=("arbitrary",) * n_grid,
                                vmem_limit_bytes=vmem_mib * 1024 * 1024)


def _sigmoid(x):
    return 1.0 / (1.0 + jnp.exp(-x))


def _rms(x, g):
    return x * lax.rsqrt(jnp.mean(x * x, axis=-1, keepdims=True) + RMS_EPS) * g


def _row_loop(n_rows, rc, fn):
    def body(i, carry):
        fn(pl.multiple_of(i * rc, rc))
        return carry
    lax.fori_loop(0, n_rows // rc, body, 0)


def _mm(x, y, precision=None):
    return jnp.dot(x, y, precision=precision, preferred_element_type=F32)


def _norm_rows_to(h_scr, x_ref, g_ref):
    g = g_ref[...]

    def chunk(r0):
        h_scr[pl.ds(r0, ROW_CHUNK), :] = _rms(x_ref[pl.ds(r0, ROW_CHUNK), :], g).astype(BF16)
    _row_loop(x_ref.shape[0], ROW_CHUNK, chunk)


def _norm_matmul_body(x_ref, g_ref, w_ref, o_ref, h_scr):
    @pl.when(pl.program_id(1) == 0)
    def _():
        _norm_rows_to(h_scr, x_ref, g_ref)

    o_ref[...] = _mm(h_scr[...], w_ref[...]).astype(o_ref.dtype)


def _norm_matmul(x, g, w, *, tm, tn, out_dtype, name, out_batch_major=False):
    d, n = w.shape
    m = x.shape[0]
    tn = min(tn, n)
    nj = n // tn
    if out_batch_major:
        out_spec = pl.BlockSpec((tm, tn), lambda i, j: (0, i * nj + j))
        out_shape = jax.ShapeDtypeStruct((tm, (m // tm) * n), out_dtype)
    else:
        out_spec = pl.BlockSpec((tm, tn), lambda i, j: (i, j))
        out_shape = jax.ShapeDtypeStruct((m, n), out_dtype)
    return pl.pallas_call(
        _norm_matmul_body,
        grid=(m // tm, nj),
        in_specs=[pl.BlockSpec((tm, d), lambda i, j: (i, 0)),
                  pl.BlockSpec((1, d), lambda i, j: (0, 0)), pl.BlockSpec((d, tn), lambda i, j: (0, j))],
        out_specs=out_spec,
        out_shape=out_shape,
        scratch_shapes=[pltpu.VMEM((tm, d), BF16)],
        compiler_params=_cparams(2),
        name=name,
    )(x, g, w)


def _mm_norm_res_body(*refs, n_a):
    a_refs = refs[:n_a]
    w_refs = refs[n_a:2 * n_a]
    g_ref, r_ref, o_ref = refs[2 * n_a:2 * n_a + 3]
    k = pl.program_id(1)

    part = None
    for a, w in zip(a_refs, w_refs):
        d = _mm(a[...].astype(BF16), w[...])
        part = d if part is None else part + d

    @pl.when(k == 0)
    def _():
        o_ref[...] = part

    @pl.when(k > 0)
    def _():
        o_ref[...] += part

    @pl.when(k == pl.num_programs(1) - 1)
    def _():
        g = g_ref[...]

        def chunk(r0):
            rs = pl.ds(r0, ROW_CHUNK)
            o_ref[rs, :] = r_ref[rs, :] + _rms(o_ref[rs, :], g)
        _row_loop(o_ref.shape[0], ROW_CHUNK, chunk)


def _mm_norm_res(a_list, w_list, g, res, *, tm, tk, name, a_batch_major=()):
    kdim, n = w_list[0].shape
    n_a = len(a_list)
    m = res.shape[0]
    tk = min(tk, kdim)
    nk = kdim // tk
    a_specs = []
    for idx in range(n_a):
        if idx < len(a_batch_major) and a_batch_major[idx]:
            a_specs.append(pl.BlockSpec((tm, tk), lambda i, k: (0, i * nk + k)))
        else:
            a_specs.append(pl.BlockSpec((tm, tk), lambda i, k: (i, k)))
    return pl.pallas_call(
        functools.partial(_mm_norm_res_body, n_a=n_a),
        grid=(m // tm, nk),
        in_specs=[*a_specs,
                  *[pl.BlockSpec((tk, n), lambda i, k: (k, 0)) for _ in w_list],
                  pl.BlockSpec((1, n), lambda i, k: (0, 0)),
                  pl.BlockSpec((tm, n), lambda i, k: (i, 0))],
        out_specs=pl.BlockSpec((tm, n), lambda i, k: (i, 0)),
        out_shape=jax.ShapeDtypeStruct((m, n), F32),
        compiler_params=_cparams(2),
        name=name,
    )(*a_list, *w_list, g, res)


def _ln_swish(y, lg, lb):
    mu = jnp.mean(y, axis=-1, keepdims=True)
    d = y - mu
    var = jnp.mean(d * d, axis=-1, keepdims=True)
    yn = d * lax.rsqrt(var + LN_EPS) * lg + lb
    return yn * _sigmoid(yn)


def _conv_prompt_body(a_ref, b_ref, w_ref, cb_ref, lg_ref, lb_ref, cv_ref, nc_ref, full, cvt, *, kw, lane_blk):
    tt, c = a_ref.shape
    past = kw - 1
    hp = -(-past // 8) * 8
    lead = hp - past
    ti = pl.program_id(1)

    @pl.when(ti == 0)
    def _():
        full[0:hp, :] = jnp.zeros((hp, c), F32)

    @pl.when(ti > 0)
    def _():
        full[0:hp, :] = full[tt:tt + hp, :]

    def glu(r0):
        dst = pl.multiple_of(hp + r0, ROW_CHUNK)
        full[pl.ds(dst, ROW_CHUNK), :] = a_ref[pl.ds(r0, ROW_CHUNK), :] * _sigmoid(b_ref[pl.ds(r0, ROW_CHUNK), :])
    _row_loop(tt, ROW_CHUNK, glu)

    def taps(r0):
        for lb in range(c // lane_blk):
            ls = slice(lb * lane_blk, (lb + 1) * lane_blk)
            win = full[pl.ds(r0, 8 + hp), ls]
            acc = jnp.broadcast_to(cb_ref[:, ls], (8, lane_blk))
            for j in range(kw):
                acc = acc + win[lead + j:lead + j + 8, :] * w_ref[j:j + 1, ls]
            cvt[pl.ds(r0, 8), ls] = acc
    _row_loop(tt, 8, taps)

    def ln(r0):
        cv_ref[pl.ds(r0, ROW_CHUNK), :] = _ln_swish(cvt[pl.ds(r0, ROW_CHUNK), :], lg_ref[...], lb_ref[...]).astype(cv_ref.dtype)
    _row_loop(tt, ROW_CHUNK, ln)

    @pl.when(ti == pl.num_programs(1) - 1)
    def _():
        nc_ref[...] = full[tt + lead:tt + hp, :]


def _conv_prompt(proj, nb, t, c, conv_w, conv_b, ln_g, ln_b, tt):
    kw = conv_w.shape[0]
    nt = t // tt
    hp = -(-(kw - 1) // 8) * 8
    vec = pl.BlockSpec((1, c), lambda b, i: (0, 0))
    return pl.pallas_call(
        functools.partial(_conv_prompt_body, kw=kw, lane_blk=min(c, 2 * LANES)),
        grid=(nb, nt),
        in_specs=[pl.BlockSpec((tt, c), lambda b, i: (b * nt + i, 0)),
                  pl.BlockSpec((tt, c), lambda b, i: (b * nt + i, 1)),
                  pl.BlockSpec((kw, c), lambda b, i: (0, 0)), vec, vec, vec],
        out_specs=[pl.BlockSpec((tt, c), lambda b, i: (b * nt + i, 0)),
                   pl.BlockSpec((None, kw - 1, c), lambda b, i: (b, 0, 0))],
        out_shape=[jax.ShapeDtypeStruct((nb * t, c), BF16), jax.ShapeDtypeStruct((nb, kw - 1, c), F32)],
        scratch_shapes=[pltpu.VMEM((tt + hp, c), F32), pltpu.VMEM((tt, c), F32)],
        compiler_params=_cparams(2),
        name="conv_prompt",
    )(proj, proj, conv_w, conv_b, ln_g, ln_b)


def _conv_sample_body(a_ref, b_ref, st_ref, w_ref, cb_ref, lg_ref, lb_ref, cv_ref, nc_ref, glu, cvt, *, t_s, kw):
    db = st_ref.shape[0]
    c = w_ref.shape[1]
    past = kw - 1
    rows = t_s * db

    def make_glu(r0):
        glu[pl.ds(r0, ROW_CHUNK), :] = a_ref[pl.ds(r0, ROW_CHUNK), :] * _sigmoid(b_ref[pl.ds(r0, ROW_CHUNK), :])
    _row_loop(rows, ROW_CHUNK, make_glu)

    def taps(r0):
        for lb in range(c // LANES):
            ls = slice(lb * LANES, (lb + 1) * LANES)
            srcs = []
            for f in range(past + t_s):
                if f < past:
                    srcs.append(st_ref[pl.ds(r0, 8), f * c + ls.start:f * c + ls.stop])
                else:
                    srcs.append(glu[pl.ds(pl.multiple_of((f - past) * db + r0, 8), 8), ls])
            for t in range(t_s):
                acc = jnp.broadcast_to(cb_ref[:, ls], (8, LANES))
                for j in range(kw):
                    acc = acc + srcs[t + j] * w_ref[j:j + 1, ls]
                cvt[pl.ds(pl.multiple_of(t * db + r0, 8), 8), ls] = acc
    _row_loop(db, 8, taps)

    def ln(r0):
        cv_ref[pl.ds(r0, ROW_CHUNK), :] = _ln_swish(cvt[pl.ds(r0, ROW_CHUNK), :], lg_ref[...], lb_ref[...]).astype(cv_ref.dtype)
    _row_loop(rows, ROW_CHUNK, ln)

    for r in range(past):
        f = t_s + r
        if f < past:
            nc_ref[:, r * c:(r + 1) * c] = st_ref[:, f * c:(f + 1) * c]
        else:
            nc_ref[:, r * c:(r + 1) * c] = glu[(f - past) * db:(f - past + 1) * db, :]


def _conv_sample(proj, state2d, db, t_s, c, conv_w, conv_b, ln_g, ln_b):
    kw = conv_w.shape[0]
    rows = t_s * db
    vec = pl.BlockSpec((1, c), lambda i: (0, 0))
    return pl.pallas_call(
        functools.partial(_conv_sample_body, t_s=t_s, kw=kw),
        grid=(1,),
        in_specs=[pl.BlockSpec((rows, c), lambda i: (0, 0)),
                  pl.BlockSpec((rows, c), lambda i: (0, 1)),
                  pl.BlockSpec((db, (kw - 1) * c), lambda i: (0, 0)),
                  pl.BlockSpec((kw, c), lambda i: (0, 0)), vec, vec, vec],
        out_specs=[pl.BlockSpec((rows, c), lambda i: (0, 0)),
                   pl.BlockSpec((db, (kw - 1) * c), lambda i: (0, 0))],
        out_shape=[jax.ShapeDtypeStruct((rows, c), BF16), jax.ShapeDtypeStruct((db, (kw - 1) * c), F32)],
        scratch_shapes=[pltpu.VMEM((rows, c), F32), pltpu.VMEM((rows, c), F32)],
        compiler_params=_cparams(1, 56),
        name="conv_sample",
    )(proj, proj, state2d, conv_w, conv_b, ln_g, ln_b)


def _head_ones(hn):
    ri = lax.broadcasted_iota(jnp.int32, (LANES, LANES), 0)
    ci = lax.broadcasted_iota(jnp.int32, (LANES, LANES), 1)
    return (ri // hn) == (ci // hn)


def _prep_body(*refs, sh, has_state, hn, dlp, alp):
    pr = refs[0:4]
    i0 = 4
    st = None
    if has_state:
        st = refs[i0:i0 + 4]
        i0 += 4
    mu = refs[i0:i0 + 4]
    w0_ref, a0_ref, kk_ref, ka_ref, wl_ref, al_ref, gl_ref = refs[i0 + 4:i0 + 11]
    r_ref, lw_ref, k2_ref, v_ref, a_ref, b_ref, g_ref = refs[i0 + 11:i0 + 18]
    bufs = refs[i0 + 18:i0 + 22]
    lact = refs[i0 + 22]
    tt, w = r_ref.shape
    off = -(-sh // 8) * 8
    ti = pl.program_id(1)

    @pl.when(ti == 0)
    def _():
        for n, bf in enumerate(bufs):
            bf[0:off, :] = st[n][...] if has_state else jnp.zeros((off, bf.shape[1]), F32)

    @pl.when(ti > 0)
    def _():
        for bf in bufs:
            bf[0:off, :] = bf[tt:tt + off, :]

    def copy(r0):
        dst = pl.multiple_of(off + r0, 8)
        for bf, p in zip(bufs, pr):
            bf[pl.ds(dst, 8), :] = p[pl.ds(r0, 8), :]
    _row_loop(tt, 8, copy)

    def shifted(bf, m, r0):
        cur = bf[pl.ds(pl.multiple_of(off + r0, 8), 8), :]
        if sh % 8 == 0:
            prev = bf[pl.ds(pl.multiple_of(off - sh + r0, 8), 8), :]
        else:
            win = bf[pl.ds(r0, 16), :]
            prev = win[8 - sh:16 - sh, :]
        return cur + (prev - cur) * m[...]

    def stage1(r0):
        r_ref[pl.ds(r0, 8), :] = shifted(bufs[0], mu[0], r0)
        k = shifted(bufs[1], mu[1], r0)
        k2_ref[pl.ds(r0, 8), :] = k
        kk = k * kk_ref[...]
        b_ref[pl.ds(r0, 8), :] = kk * kk
        v_ref[pl.ds(r0, 8), :] = shifted(bufs[2], mu[2], r0)
    _row_loop(tt, 8, stage1)

    def stage1l(r0):
        xl = None
        for h in range(2):
            part = shifted(bufs[3], mu[3], r0 + 8 * h)
            xl = part if xl is None else jnp.concatenate([xl, part], axis=0)
        lane = lax.broadcasted_iota(jnp.int32, xl.shape, 1)
        act = jnp.where(lane < dlp, jnp.tanh(xl), jnp.where(lane < dlp + alp, xl, _sigmoid(xl)))
        lact[pl.ds(r0, ROW_CHUNK), :] = act.astype(BF16)
    _row_loop(tt, ROW_CHUNK, stage1l)

    lw_ref[...] = _mm(lact[:, 0:dlp], wl_ref[...])
    a_ref[...] = _mm(lact[:, dlp:dlp + alp], al_ref[...])
    g_ref[...] = _mm(lact[:, dlp + alp:], gl_ref[...])
    ones = jnp.where(_head_ones(hn), 1.0, 0.0).astype(F32)
    for q in range(w // LANES):
        ls = slice(q * LANES, (q + 1) * LANES)
        b_ref[:, ls] = _mm(b_ref[:, ls], ones, HIGHEST)

    def stage3(r0):
        rs = pl.ds(r0, 8)
        k = k2_ref[rs, :]
        lw_ref[rs, :] = -DECAY_SCALE * _sigmoid(w0_ref[...] + lw_ref[rs, :])
        asig = _sigmoid(a0_ref[...] + a_ref[rs, :])
        kk = k * kk_ref[...]
        kkn = kk / jnp.maximum(jnp.sqrt(b_ref[rs, :]), 1e-12)
        k2_ref[rs, :] = k * (1.0 + (asig - 1.0) * ka_ref[...])
        a_ref[rs, :] = -kkn
        b_ref[rs, :] = kkn * asig
    _row_loop(tt, 8, stage3)


def _prep(proj, st, mu_p, w0, a0, k_k, k_a, wl, al, gl, *, n_seq_tiles, tiles_per_seq, tt, c, w, lp, sh, hn,
          out_batch_major=False):
    dlp, alp = wl.shape[0], al.shape[0]
    has_state = st is not None
    rows = n_seq_tiles * tiles_per_seq * tt
    cb = 2 * c // w
    lb = (2 * c + 3 * w) // lp
    off = -(-sh // 8) * 8
    rmap = lambda o: (lambda s, i: (s * tiles_per_seq + i, o))
    in_specs = [pl.BlockSpec((tt, w), rmap(cb)), pl.BlockSpec((tt, w), rmap(cb + 1)),
                pl.BlockSpec((tt, w), rmap(cb + 2)), pl.BlockSpec((tt, lp), rmap(lb))]
    args = [proj, proj, proj, proj]
    if has_state:
        in_specs += [pl.BlockSpec((sh, w), lambda s, i: (0, 0)), pl.BlockSpec((sh, w), lambda s, i: (0, 1)),
                     pl.BlockSpec((sh, w), lambda s, i: (0, 2)), pl.BlockSpec((sh, lp), lambda s, i: (0, 3 * w // lp))]
        args += [st, st, st, st]
    in_specs += [pl.BlockSpec((1, w), lambda s, i: (0, 0)), pl.BlockSpec((1, w), lambda s, i: (0, 1)),
                 pl.BlockSpec((1, w), lambda s, i: (0, 2)), pl.BlockSpec((1, lp), lambda s, i: (0, 3 * w // lp))]
    args += [mu_p, mu_p, mu_p, mu_p]
    vec = pl.BlockSpec((1, w), lambda s, i: (0, 0))
    in_specs += [vec, vec, vec, vec,
                 pl.BlockSpec(wl.shape, lambda s, i: (0, 0)), pl.BlockSpec(al.shape, lambda s, i: (0, 0)),
                 pl.BlockSpec(gl.shape, lambda s, i: (0, 0))]
    args += [w0, a0, k_k, k_a, wl, al, gl]
    if out_batch_major:
        assert n_seq_tiles == 1
        ospec = pl.BlockSpec((tt, w), lambda s, i: (0, i))
        oshape = jax.ShapeDtypeStruct((tt, tiles_per_seq * w), F32)
    else:
        ospec = pl.BlockSpec((tt, w), lambda s, i: (s * tiles_per_seq + i, 0))
        oshape = jax.ShapeDtypeStruct((rows, w), F32)
    return pl.pallas_call(
        functools.partial(_prep_body, sh=sh, has_state=has_state, hn=hn, dlp=dlp, alp=alp),
        grid=(n_seq_tiles, tiles_per_seq),
        in_specs=in_specs,
        out_specs=[ospec] * 7,
        out_shape=[oshape] * 7,
        scratch_shapes=[pltpu.VMEM((off + tt, w), F32)] * 3 + [pltpu.VMEM((off + tt, lp), F32), pltpu.VMEM((tt, lp), BF16)],
        compiler_params=_cparams(2, 56),
        name="rwkv_prep_state" if has_state else "rwkv_prep",
    )(*args)


def _wkv_body(*refs, chunk, pp, has_state, hn):
    r_ref, lw_ref, k_ref, v_ref, a_ref, b_ref, g_ref, rk_ref, lg_ref, lb_ref = refs[:10]
    i0 = 10
    s0_ref = None
    if has_state:
        s0_ref = refs[i0]
        i0 += 1
    o_ref, so_ref, sbd = refs[i0:i0 + 3]
    ell = chunk
    l2 = 2 * ell
    n_iter = max(1, int(math.log2(ell)))
    c = pl.program_id(2)
    bd = _head_ones(hn)
    lane = lax.broadcasted_iota(jnp.int32, (1, LANES), 1)
    lo = lane < hn

    @pl.when(c == 0)
    def _():
        if has_state:
            ri = lax.broadcasted_iota(jnp.int32, (hn, LANES), 0)
            ci = lax.broadcasted_iota(jnp.int32, (hn, LANES), 1)
            dup = jnp.where((ci % hn) == ri, 1.0, 0.0).astype(F32)
            for q in range(pp):
                sbd[q] = jnp.where(bd, _mm(s0_ref[q], dup, HIGHEST), 0.0)
        else:
            sbd[...] = jnp.zeros(sbd.shape, F32)

    lw = lw_ref[...]
    ri = lax.broadcasted_iota(jnp.int32, (ell, ell), 0)
    ci = lax.broadcasted_iota(jnp.int32, (ell, ell), 1)
    tri = jnp.where(ri >= ci, 1.0, 0.0).astype(F32)
    cs = _mm(tri, lw, HIGHEST)
    w_in = jnp.exp(cs)
    w_inv = jnp.exp(-cs)
    a_t = a_ref[...] * jnp.exp(cs - lw)
    r_t = r_ref[...] * w_in
    b_t = b_ref[...] * w_inv
    k_t = k_ref[...] * w_inv
    w_last = w_in[ell - 1:ell, :]
    rr = lax.broadcasted_iota(jnp.int32, (l2, l2), 0) % ell
    cc = lax.broadcasted_iota(jnp.int32, (l2, l2), 1) % ell
    strict = cc < rr
    incl = cc <= rr
    ones = jnp.where(bd, 1.0, 0.0).astype(F32)
    avg = jnp.where(bd, 1.0 / hn, 0.0).astype(F32)

    def nt(x, y):
        return lax.dot_general(x, y, _NT, precision=HIGHEST, preferred_element_type=F32)

    for q in range(pp):
        ls = slice(q * LANES, (q + 1) * LANES)

        def blk(x):
            xs = x[:, ls]
            return jnp.concatenate([jnp.where(lo, xs, 0.0), jnp.where(lo, 0.0, xs)], axis=0)

        vq = v_ref[:, ls]
        ab, rb, bb, kb, vb = blk(a_t), blk(r_t), blk(b_t), blk(k_t), blk(v_ref[...])
        s = sbd[q]
        m_ab = jnp.where(strict, nt(ab, bb), 0.0)
        m_ak = jnp.where(strict, nt(ab, kb), 0.0)
        m_rb = jnp.where(incl, nt(rb, bb), 0.0)
        m_rk = jnp.where(incl, nt(rb, kb), 0.0)
        u = nt(ab, s) + _mm(m_ak, vb, HIGHEST)
        for it in range(n_iter):
            u = u + _mm(m_ab, u, HIGHEST)
            if it < n_iter - 1:
                m_ab = _mm(m_ab, m_ab, HIGHEST)
        yb = nt(rb, s) + _mm(m_rb, u, HIGHEST) + _mm(m_rk, vb, HIGHEST)
        y = yb[0:ell, :] + yb[ell:l2, :]
        upd = lax.dot_general(jnp.concatenate([u, vb], axis=0), jnp.concatenate([bb, kb], axis=0), _TN,
                              precision=HIGHEST, preferred_element_type=F32)
        sbd[q] = (s + upd) * w_last[:, ls]

        mu = _mm(y, avg, HIGHEST)
        d = y - mu
        var = _mm(d * d, avg, HIGHEST)
        yn = d * lax.rsqrt(var + GN_EPS) * lg_ref[:, ls] + lb_ref[:, ls]
        bonus = _mm(r_ref[:, ls] * k_ref[:, ls] * rk_ref[:, ls], ones, HIGHEST) * vq
        o_ref[:, ls] = ((yn + bonus) * g_ref[:, ls]).astype(o_ref.dtype)

    @pl.when(c == pl.num_programs(2) - 1)
    def _():
        ri2 = lax.broadcasted_iota(jnp.int32, (LANES, hn), 0)
        ci2 = lax.broadcasted_iota(jnp.int32, (LANES, hn), 1)
        fold = jnp.where((ri2 % hn) == ci2, 1.0, 0.0).astype(F32)
        for q in range(pp):
            so_ref[q] = _mm(sbd[q], fold, HIGHEST)


def _wkv(streams, r_k, ln_g, ln_b, s0, *, nb, n_chunks, chunk, w, hn, pp, out_dtype):
    pw = pp * LANES
    ng = w // pw
    has_state = s0 is not None
    sspec = pl.BlockSpec((chunk, pw), lambda b, g, c: (b * n_chunks + c, g))
    out_shape0 = jax.ShapeDtypeStruct((nb * n_chunks * chunk, w), out_dtype)
    vec = pl.BlockSpec((1, pw), lambda b, g, c: (0, g))
    stspec = pl.BlockSpec((None, pp, LANES, hn), lambda b, g, c: (b, g, 0, 0))
    in_specs = [sspec] * 7 + [vec] * 3 + ([stspec] if has_state else [])
    args = list(streams) + [r_k, ln_g, ln_b] + ([s0] if has_state else [])
    out, s_new = pl.pallas_call(
        functools.partial(_wkv_body, chunk=chunk, pp=pp, has_state=has_state, hn=hn),
        grid=(nb, ng, n_chunks),
        in_specs=in_specs,
        out_specs=[sspec, stspec],
        out_shape=[out_shape0, jax.ShapeDtypeStruct((nb, w // LANES, LANES, hn), F32)],
        scratch_shapes=[pltpu.VMEM((pp, LANES, LANES), F32)],
        compiler_params=_cparams(3),
        name="wkv_state" if has_state else "wkv",
    )(*args)
    return out, s_new


def _attn_body(q_ref, k_ref, v_ref, o_ref, *scr, nh, scale, cache_kv):
    if cache_kv:
        kb_ref, vb_ref = scr

        @pl.when(pl.program_id(1) == 0)
        def _():
            kb_ref[...] = k_ref[...].astype(BF16)
            vb_ref[...] = v_ref[...].astype(BF16)
    hd = q_ref.shape[1] // nh
    for h in range(nh):
        hs = slice(h * hd, (h + 1) * hd)
        if cache_kv:
            kh, vh = kb_ref[:, hs], vb_ref[:, hs]
        else:
            kh, vh = k_ref[:, hs].astype(BF16), v_ref[:, hs].astype(BF16)
        s = lax.dot_general(q_ref[:, hs].astype(BF16), kh, _NT, preferred_element_type=F32) * scale
        p = jnp.exp(s - jnp.max(s, axis=-1, keepdims=True))
        p = p / jnp.sum(p, axis=-1, keepdims=True)
        o_ref[:, hs] = _mm(p.astype(BF16), vh).astype(o_ref.dtype)


def _attn_prompt(q, k, v, nb, t, nh, tq):
    nm, xd = k.shape[1], k.shape[2]
    nt = t // tq
    kv = pl.BlockSpec((None, nm, xd), lambda b, i: (b, 0, 0))
    return pl.pallas_call(
        functools.partial(_attn_body, nh=nh, scale=(xd // nh) ** -0.5, cache_kv=True),
        grid=(nb, nt),
        in_specs=[pl.BlockSpec((tq, xd), lambda b, i: (b * nt + i, 0)), kv, kv],
        out_specs=pl.BlockSpec((tq, xd), lambda b, i: (b * nt + i, 0)),
        out_shape=jax.ShapeDtypeStruct((nb * t, xd), BF16),
        scratch_shapes=[pltpu.VMEM((nm, xd), BF16)] * 2,
        compiler_params=_cparams(2),
        name="attn_prompt",
    )(q, k, v)


def _attn_sample(q, k, v, db, t_s, nh):
    nm, xd = k.shape[1], k.shape[2]
    qs = pl.BlockSpec((t_s, xd), lambda b: (b, 0))
    kv = pl.BlockSpec((None, nm, xd), lambda b: (b, 0, 0))
    return pl.pallas_call(
        functools.partial(_attn_body, nh=nh, scale=(xd // nh) ** -0.5, cache_kv=False),
        grid=(db,),
        in_specs=[qs, kv, kv],
        out_specs=qs,
        out_shape=jax.ShapeDtypeStruct((db * t_s, xd), F32),
        compiler_params=_cparams(1),
        name="attn_sample",
    )(q, k, v)


def _ffn_body(*refs, sh, t_s, has_state, tiles_per_seq):
    x_ref, gpre_ref, wua_ref, wub_ref, cw_ref, cb_ref, wd_ref, gpost_ref = refs[:8]
    i0 = 8
    st = None
    if has_state:
        st = refs[i0:i0 + 4]
        i0 += 4
    o_ref = refs[i0]
    i0 += 1
    n_nf = 4 if has_state else 1
    nf_refs = refs[i0:i0 + n_nf]
    i0 += n_nf
    hn_scr, fa, fb, act = refs[i0:i0 + 4]
    carry = None if has_state else refs[i0 + 4]
    tm, d = x_ref.shape
    tn = fa.shape[1]
    off = max(8, 2 * sh)
    n_t = o_ref.shape[1] // d
    rows = tm // n_t
    i = pl.program_id(0)
    j = pl.program_id(1)

    @pl.when(j == 0)
    def _():
        _norm_rows_to(hn_scr, x_ref, gpre_ref)

    if has_state:
        fa[0:sh, :] = st[0][...]
        fb[0:sh, :] = st[1][...]
        fa[sh:2 * sh, :] = st[2][...]
        fb[sh:2 * sh, :] = st[3][...]
    else:
        first = (i % tiles_per_seq) == 0

        @pl.when(first)
        def _():
            fa[0:off, :] = jnp.zeros((off, tn), F32)
            fb[0:off, :] = jnp.zeros((off, tn), F32)

        @pl.when(jnp.logical_not(first))
        def _():
            fa[0:off, :] = carry[j, :, 0:tn]
            fb[0:off, :] = carry[j, :, tn:2 * tn]

    fa[off:off + tm, :] = _mm(hn_scr[...], wua_ref[...])
    fb[off:off + tm, :] = _mm(hn_scr[...], wub_ref[...])

    if not has_state:
        carry[j, :, 0:tn] = fa[tm:tm + off, :]
        carry[j, :, tn:2 * tn] = fb[tm:tm + off, :]

    def conv_act(r0):
        def conv(f, h):
            if sh % 8 == 0:
                s2 = f[pl.ds(pl.multiple_of(off - 2 * sh + r0, ROW_CHUNK), ROW_CHUNK), :]
                s1 = f[pl.ds(pl.multiple_of(off - sh + r0, ROW_CHUNK), ROW_CHUNK), :]
                s0 = f[pl.ds(pl.multiple_of(off + r0, ROW_CHUNK), ROW_CHUNK), :]
            else:
                win = f[pl.ds(r0, ROW_CHUNK + 8), :]
                s2 = win[8 - 2 * sh:8 - 2 * sh + ROW_CHUNK, :]
                s1 = win[8 - sh:8 - sh + ROW_CHUNK, :]
                s0 = win[8:8 + ROW_CHUNK, :]
            return cb_ref[h:h + 1, :] + s2 * cw_ref[0, h:h + 1, :] + s1 * cw_ref[1, h:h + 1, :] + s0 * cw_ref[2, h:h + 1, :]
        ua = conv(fa, 0)
        ub = conv(fb, 1)
        act[pl.ds(r0, ROW_CHUNK), :] = (ua * _sigmoid(ua) * ub).astype(BF16)
    _row_loop(tm, ROW_CHUNK, conv_act)

    part = _mm(act[...], wd_ref[...])

    @pl.when(j == 0)
    def _():
        for t in range(n_t):
            o_ref[:, t * d:(t + 1) * d] = part[t * rows:(t + 1) * rows, :]

    @pl.when(j > 0)
    def _():
        for t in range(n_t):
            o_ref[:, t * d:(t + 1) * d] += part[t * rows:(t + 1) * rows, :]

    if has_state:
        for r in range(2):
            src = off + (t_s - 2 + r) * sh
            nf_refs[2 * r][...] = fa[src:src + sh, :]
            nf_refs[2 * r + 1][...] = fb[src:src + sh, :]
    else:
        nf = nf_refs[0]
        for r in range(2):
            src = off + tm - 2 + r
            nf[r, 0:1, :] = fa[src:src + 1, :]
            nf[r, 1:2, :] = fb[src:src + 1, :]

    @pl.when(j == pl.num_programs(1) - 1)
    def _():
        g = gpost_ref[...]
        for t in range(n_t):
            def chunk(r0, t=t):
                src = pl.multiple_of(t * rows + r0, ROW_CHUNK)
                y = o_ref[pl.ds(r0, ROW_CHUNK), t * d:(t + 1) * d]
                o_ref[pl.ds(r0, ROW_CHUNK), t * d:(t + 1) * d] = x_ref[pl.ds(src, ROW_CHUNK), :] + _rms(y, g)
            _row_loop(rows, ROW_CHUNK, chunk)


def _ffn_prompt(x, g_pre, w_up, cw3, cb2, w_down, g_post, *, nb, t, tm, tn):
    d = x.shape[1]
    dff = w_down.shape[0]
    nj = dff // tn
    tps = t // tm
    vec = pl.BlockSpec((1, d), lambda i, j: (0, 0))
    out, nf = pl.pallas_call(
        functools.partial(_ffn_body, sh=1, t_s=None, has_state=False, tiles_per_seq=tps),
        grid=(nb * tps, nj),
        in_specs=[pl.BlockSpec((tm, d), lambda i, j: (i, 0)), vec,
                  pl.BlockSpec((d, tn), lambda i, j: (0, j)), pl.BlockSpec((d, tn), lambda i, j: (0, j + nj)),
                  pl.BlockSpec((3, 2, tn), lambda i, j: (0, 0, j)), pl.BlockSpec((2, tn), lambda i, j: (0, j)),
                  pl.BlockSpec((tn, d), lambda i, j: (j, 0)), vec],
        out_specs=[pl.BlockSpec((tm, d), lambda i, j: (i, 0)),
                   pl.BlockSpec((None, 2, 2, tn), lambda i, j: (i, 0, 0, j))],
        out_shape=[jax.ShapeDtypeStruct((nb * t, d), F32), jax.ShapeDtypeStruct((nb * tps, 2, 2, dff), F32)],
        scratch_shapes=[pltpu.VMEM((tm, d), BF16),
                        pltpu.VMEM((8 + tm, tn), F32), pltpu.VMEM((8 + tm, tn), F32),
                        pltpu.VMEM((tm, tn), BF16), pltpu.VMEM((nj, 8, 2 * tn), F32)],
        compiler_params=_cparams(2, 56),
        name="ffn_prompt",
    )(x, g_pre, w_up, w_up, cw3, cb2, w_down, g_post)
    return out, nf.reshape(nb, tps, 2, 2 * dff)[:, -1]


def _ffn_sample(x_tm, st2d, g_pre, w_up, cw3, cb2, w_down, g_post, *, db, t_s, tn):
    d = x_tm.shape[1]
    dff = w_down.shape[0]
    nj = dff // tn
    tm = t_s * db
    vec = pl.BlockSpec((1, d), lambda i, j: (0, 0))
    st_specs = [pl.BlockSpec((db, tn), (lambda i, j, o=o: (0, o * nj + j))) for o in range(4)]
    nf_spec = pl.BlockSpec((db, tn), lambda i, j: (0, j))
    out, n0a, n0b, n1a, n1b = pl.pallas_call(
        functools.partial(_ffn_body, sh=db, t_s=t_s, has_state=True, tiles_per_seq=1),
        grid=(1, nj),
        in_specs=[pl.BlockSpec((tm, d), lambda i, j: (0, 0)), vec,
                  pl.BlockSpec((d, tn), lambda i, j: (0, j)), pl.BlockSpec((d, tn), lambda i, j: (0, j + nj)),
                  pl.BlockSpec((3, 2, tn), lambda i, j: (0, 0, j)), pl.BlockSpec((2, tn), lambda i, j: (0, j)),
                  pl.BlockSpec((tn, d), lambda i, j: (j, 0)), vec, *st_specs],
        out_specs=[pl.BlockSpec((db, t_s * d), lambda i, j: (0, 0)), nf_spec, nf_spec, nf_spec, nf_spec],
        out_shape=[jax.ShapeDtypeStruct((db, t_s * d), F32)] + [jax.ShapeDtypeStruct((db, dff), F32)] * 4,
        scratch_shapes=[pltpu.VMEM((tm, d), BF16),
                        pltpu.VMEM((2 * db + tm, tn), F32), pltpu.VMEM((2 * db + tm, tn), F32),
                        pltpu.VMEM((tm, tn), BF16)],
        compiler_params=_cparams(2, 56),
        name="ffn_sample",
    )(x_tm, g_pre, w_up, w_up, cw3, cb2, w_down, g_post, st2d, st2d, st2d, st2d)
    new_ffn = jnp.stack([jnp.concatenate([n0a, n0b], axis=1), jnp.concatenate([n1a, n1b], axis=1)], axis=1)
    return out, new_ffn


def _pad_cols(x, width):
    return jnp.pad(x, [(0, 0)] * (x.ndim - 1) + [(0, width - x.shape[-1])])


def _round_up(n, m):
    return -(-n // m) * m


def kernel(x_prompt, x_sample, cache_mem_k, cache_mem_v, state_conv, state_shift, state_wkv, state_ffn, mem_prompt, norm_mix_pre, w_in, conv_dw, conv_dw_b, conv_ln_g, conv_ln_b, rwkv_mu, w0, w_lora, a0, a_lora, g_lora, k_k, k_a, r_k, ln_x_g, ln_x_b, w_out, norm_mix_post, norm_xa_pre, norm_mem, w_q, w_k, w_v, w_o, norm_xa_post, norm_ffn_pre, w_up, ffn_dw, ffn_dw_b, w_down, norm_ffn_post):
    nb, t, d = x_prompt.shape
    db, t_s, _ = x_sample.shape
    depth = w_in.shape[0]
    c = conv_dw.shape[-1]
    w = w0.shape[-1]
    n_heads, hn = state_wkv.shape[2], state_wkv.shape[3]
    dl, al, gl = w_lora.shape[1], a_lora.shape[1], g_lora.shape[1]
    dlp, alp, glp = _round_up(dl, LANES), _round_up(al, LANES), _round_up(gl, LANES)
    lp = dlp + alp + glp
    n_mem, xa_heads, xa_hd = cache_mem_k.shape[2:]
    xd = xa_heads * xa_hd
    dff = w_down.shape[1]
    kc = conv_dw.shape[1]
    assert depth == 1 and c == w and (2 * c + 3 * w) % lp == 0 and (3 * w) % lp == 0 and hn * 2 == LANES

    def pad_rcols(x):
        o = 3 * w
        return jnp.concatenate([x[..., :o], _pad_cols(x[..., o:o + dl], dlp), _pad_cols(x[..., o + dl:o + dl + al], alp),
                                _pad_cols(x[..., o + dl + al:], glp)], axis=-1)

    def unpad_rcols(x):
        o = 3 * w
        return jnp.concatenate([x[..., :o + dl], x[..., o + dlp:o + dlp + al], x[..., o + dlp + alp:o + dlp + alp + gl]], axis=-1)

    row = lambda v: v.reshape(1, -1)
    l = 0
    w_in_p = jnp.concatenate([w_in[l][:, :2 * c], pad_rcols(w_in[l][:, 2 * c:])], axis=1).astype(BF16)
    mu_p = pad_rcols(rwkv_mu[l]).reshape(1, -1)
    wl_p = jnp.pad(w_lora[l], ((0, dlp - dl), (0, 0))).astype(BF16)
    al_p = jnp.pad(a_lora[l], ((0, alp - al), (0, 0))).astype(BF16)
    gl_p = jnp.pad(g_lora[l], ((0, glp - gl), (0, 0))).astype(BF16)
    w_out_b, w_q_b, w_k_b, w_v_b, w_o_b = (x[l].astype(BF16) for x in (w_out, w_q, w_k, w_v, w_o))
    w_up_b, w_down_b = w_up[l].astype(BF16), w_down[l].astype(BF16)
    cw3 = ffn_dw[l].reshape(ffn_dw.shape[1], 2, dff)
    cb2 = ffn_dw_b[l].reshape(2, dff)
    g_mix_pre, g_mix_post, g_xa_pre, g_mem, g_xa_post, g_ffn_pre, g_ffn_post = (
        row(x[l]) for x in (norm_mix_pre, norm_mix_post, norm_xa_pre, norm_mem, norm_xa_post, norm_ffn_pre, norm_ffn_post))
    prep_params = (row(w0[l]), row(a0[l]), row(k_k[l]), row(k_a[l]), wl_p, al_p, gl_p)
    wkv_params = (row(r_k[l]), row(ln_x_g[l]), row(ln_x_b[l]))
    conv_params = (conv_dw[l], row(conv_dw_b[l]), row(conv_ln_g[l]), row(conv_ln_b[l]))

    tl = _TILES
    tm_a = min(tl["tm_a"], nb * t)
    tn_a = tl["tn_a"]
    tm_e = min(tl["tm_e"], nb * t)
    tk_e = tl["tk_e"]
    chunk = min(tl["chunk"], t)
    pp = min(tl["pp"], w // LANES)
    tn_f = min(tl["tn_f"], dff)

    n_mem_rows = nb * n_mem
    mem2d = mem_prompt.reshape(n_mem_rows, d)
    mk = _norm_matmul(mem2d, g_mem, w_k_b, tm=min(tl["tm_a"], n_mem_rows), tn=tn_a, out_dtype=F32, name="mem_k")
    mv = _norm_matmul(mem2d, g_mem, w_v_b, tm=min(tl["tm_a"], n_mem_rows), tn=tn_a, out_dtype=F32, name="mem_v")
    xp = x_prompt.reshape(nb * t, d)
    proj = _norm_matmul(xp, g_mix_pre, w_in_p, tm=tm_a, tn=tn_a, out_dtype=F32, name="proj_prompt")
    cv, conv_p = _conv_prompt(proj, nb, t, c, *conv_params, tt=min(tl["tt_conv"], t))
    tt_p = min(tl["tt_prep"], t)
    streams = _prep(proj, None, mu_p, *prep_params, n_seq_tiles=nb, tiles_per_seq=t // tt_p, tt=tt_p, c=c, w=w, lp=lp, sh=1, hn=hn)
    rw, wkv_p = _wkv(streams, *wkv_params, None, nb=nb, n_chunks=t // chunk, chunk=chunk, w=w, hn=hn, pp=pp, out_dtype=BF16)
    x1 = _mm_norm_res([cv, rw], [w_out_b[:c], w_out_b[c:]], g_mix_post, xp, tm=tm_e, tk=tk_e, name="mix_out_prompt")
    q = _norm_matmul(x1, g_xa_pre, w_q_b, tm=tm_a, tn=tn_a, out_dtype=BF16, name="q_prompt")
    o = _attn_prompt(q, mk.reshape(nb, n_mem, xd), mv.reshape(nb, n_mem, xd), nb, t, xa_heads, tq=min(tl["tq"], t))
    x2 = _mm_norm_res([o], [w_o_b], g_xa_post, x1, tm=tm_e, tk=tk_e, name="attn_out_prompt")
    yp, ffn_p = _ffn_prompt(x2, g_ffn_pre, w_up_b, cw3, cb2, w_down_b, g_ffn_post, nb=nb, t=t, tm=min(tl["tm_f"], t), tn=tn_f)
    shift_p = unpad_rcols(proj.reshape(nb, t, -1)[:, -1, 2 * c:])

    rows_s = t_s * db
    xs_tm = jnp.swapaxes(x_sample, 0, 1).reshape(rows_s, d)
    proj_s = _norm_matmul(xs_tm, g_mix_pre, w_in_p, tm=rows_s, tn=tn_a, out_dtype=F32, name="proj_sample")
    cv_s, conv_s = _conv_sample(proj_s, state_conv[l].reshape(db, (kc - 1) * c), db, t_s, c, *conv_params)
    streams_s = _prep(proj_s, pad_rcols(state_shift[l]), mu_p, *prep_params, n_seq_tiles=1, tiles_per_seq=t_s, tt=db,
                      c=c, w=w, lp=lp, sh=db, hn=hn, out_batch_major=True)
    streams_s = [x.reshape(rows_s, w) for x in streams_s]
    rw_s, wkv_s = _wkv(streams_s, *wkv_params, state_wkv[l].reshape(db, w // LANES, LANES, hn), nb=db, n_chunks=1, chunk=t_s,
                       w=w, hn=hn, pp=pp, out_dtype=F32)
    x1_s = _mm_norm_res([cv_s, rw_s.reshape(db, t_s * w)], [w_out_b[:c], w_out_b[c:]], g_mix_post, xs_tm, tm=db, tk=tk_e,
                        name="mix_out_sample", a_batch_major=(False, True))
    q_s = _norm_matmul(x1_s, g_xa_pre, w_q_b, tm=db, tn=tn_a, out_dtype=F32, name="q_sample", out_batch_major=True)
    o_s = _attn_sample(q_s.reshape(rows_s, xd), cache_mem_k[l].reshape(db, n_mem, xd), cache_mem_v[l].reshape(db, n_mem, xd),
                       db, t_s, xa_heads)
    x2_s = _mm_norm_res([o_s.reshape(db, t_s * xd)], [w_o_b], g_xa_post, x1_s, tm=db, tk=tk_e, name="attn_out_sample",
                        a_batch_major=(True,))
    ys2, ffn_s = _ffn_sample(x2_s, state_ffn[l].reshape(db, 4 * dff), g_ffn_pre, w_up_b, cw3, cb2, w_down_b, g_ffn_post,
                             db=db, t_s=t_s, tn=tn_f)
    shift_s = unpad_rcols(proj_s[(t_s - 1) * db:, 2 * c:])

    return (yp.reshape(nb, t, d), ys2.reshape(db, t_s, d),
            conv_p[None], conv_s.reshape(db, kc - 1, c)[None],
            shift_p[None], shift_s[None],
            wkv_p.reshape(nb, n_heads, hn, hn)[None], wkv_s.reshape(db, n_heads, hn, hn)[None],
            ffn_p[None], ffn_s[None],
            mk.reshape(nb, n_mem, xa_heads, xa_hd)[None], mv.reshape(nb, n_mem, xa_heads, xa_hd)[None])
```

```python
import functools
import math

import jax
import jax.numpy as jnp
from jax import lax
from jax.experimental import pallas as pl
from jax.experimental.pallas import tpu as pltpu

F32 = jnp.float32
BF16 = jnp.bfloat16
RMS_EPS = 1e-6
LN_EPS = 1e-5
GN_EPS = 64e-5
LANES = 128
ROW_CHUNK = 16
HIGHEST = lax.Precision.HIGHEST
DECAY_SCALE = math.exp(-0.5)
_NT = (((1,), (1,)), ((), ()))
_TN = (((0,), (0,)), ((), ()))


_TILES = dict(
    tm_a=1024, tn_a=512,
    tm_e=512, tk_e=512,
    tt_conv=256, tt_prep=256,
    chunk=64, pp=8,
    tq=256,
    tm_f=512, tn_f=512,
)


def _cparams(n_grid, vmem_mib=48):
    return pltpu.CompilerParams(dimension_semantics=("arbitrary",) * n_grid,
                                vmem_limit_bytes=vmem_mib * 1024 * 1024)


def _sigmoid(x):
    return 1.0 / (1.0 + jnp.exp(-x))


def _rms(x, g):
    return x * lax.rsqrt(jnp.mean(x * x, axis=-1, keepdims=True) + RMS_EPS) * g


def _row_loop(n_rows, rc, fn, unroll=1):
    def body(i, carry):
        fn(pl.multiple_of(i * rc, rc))
        return carry
    lax.fori_loop(0, n_rows // rc, body, 0, unroll=unroll)


def _mm(x, y, precision=None):
    return jnp.dot(x, y, precision=precision, preferred_element_type=F32)


def _norm_rows_to(h_scr, x_ref, g_ref):
    g = g_ref[...]

    def chunk(r0):
        h_scr[pl.ds(r0, ROW_CHUNK), :] = _rms(x_ref[pl.ds(r0, ROW_CHUNK), :], g).astype(BF16)
    _row_loop(x_ref.shape[0], ROW_CHUNK, chunk)


def _norm_matmul_body(x_ref, g_ref, w_ref, o_ref, h_scr):
    @pl.when(pl.program_id(1) == 0)
    def _():
        _norm_rows_to(h_scr, x_ref, g_ref)

    o_ref[...] = _mm(h_scr[...], w_ref[...]).astype(o_ref.dtype)


def _norm_matmul(x, g, w, *, tm, tn, out_dtype, name, out_batch_major=False):
    d, n = w.shape
    m = x.shape[0]
    tn = min(tn, n)
    nj = n // tn
    if out_batch_major:
        out_spec = pl.BlockSpec((tm, tn), lambda i, j: (0, i * nj + j))
        out_shape = jax.ShapeDtypeStruct((tm, (m // tm) * n), out_dtype)
    else:
        out_spec = pl.BlockSpec((tm, tn), lambda i, j: (i, j))
        out_shape = jax.ShapeDtypeStruct((m, n), out_dtype)
    return pl.pallas_call(
        _norm_matmul_body,
        grid=(m // tm, nj),
        in_specs=[pl.BlockSpec((tm, d), lambda i, j: (i, 0)),
                  pl.BlockSpec((1, d), lambda i, j: (0, 0)), pl.BlockSpec((d, tn), lambda i, j: (0, j))],
        out_specs=out_spec,
        out_shape=out_shape,
        scratch_shapes=[pltpu.VMEM((tm, d), BF16)],
        compiler_params=_cparams(2),
        name=name,
    )(x, g, w)


def _mm_norm_res_body(*refs, n_a):
    a_refs = refs[:n_a]
    w_refs = refs[n_a:2 * n_a]
    g_ref, r_ref, o_ref = refs[2 * n_a:2 * n_a + 3]
    k = pl.program_id(1)

    part = None
    for a, w in zip(a_refs, w_refs):
        d = _mm(a[...].astype(BF16), w[...])
        part = d if part is None else part + d

    @pl.when(k == 0)
    def _():
        o_ref[...] = part

    @pl.when(k > 0)
    def _():
        o_ref[...] += part

    @pl.when(k == pl.num_programs(1) - 1)
    def _():
        g = g_ref[...]

        def chunk(r0):
            rs = pl.ds(r0, ROW_CHUNK)
            o_ref[rs, :] = r_ref[rs, :] + _rms(o_ref[rs, :], g)
        _row_loop(o_ref.shape[0], ROW_CHUNK, chunk)


def _mm_norm_res(a_list, w_list, g, res, *, tm, tk, name, a_batch_major=()):
    kdim, n = w_list[0].shape
    n_a = len(a_list)
    m = res.shape[0]
    tk = min(tk, kdim)
    nk = kdim // tk
    a_specs = []
    for idx in range(n_a):
        if idx < len(a_batch_major) and a_batch_major[idx]:
            a_specs.append(pl.BlockSpec((tm, tk), lambda i, k: (0, i * nk + k)))
        else:
            a_specs.append(pl.BlockSpec((tm, tk), lambda i, k: (i, k)))
    return pl.pallas_call(
        functools.partial(_mm_norm_res_body, n_a=n_a),
        grid=(m // tm, nk),
        in_specs=[*a_specs,
                  *[pl.BlockSpec((tk, n), lambda i, k: (k, 0)) for _ in w_list],
                  pl.BlockSpec((1, n), lambda i, k: (0, 0)),
                  pl.BlockSpec((tm, n), lambda i, k: (i, 0))],
        out_specs=pl.BlockSpec((tm, n), lambda i, k: (i, 0)),
        out_shape=jax.ShapeDtypeStruct((m, n), F32),
        compiler_params=_cparams(2),
        name=name,
    )(*a_list, *w_list, g, res)


def _ln_swish(y, lg, lb):
    mu = jnp.mean(y, axis=-1, keepdims=True)
    d = y - mu
    var = jnp.mean(d * d, axis=-1, keepdims=True)
    yn = d * lax.rsqrt(var + LN_EPS) * lg + lb
    return yn * _sigmoid(yn)


def _conv_prompt_body(a_ref, b_ref, w_ref, cb_ref, lg_ref, lb_ref, cv_ref, nc_ref, full, cvt, *, kw, lane_blk):
    tt, c = a_ref.shape
    past = kw - 1
    hp = -(-past // 8) * 8
    lead = hp - past
    ti = pl.program_id(1)

    @pl.when(ti == 0)
    def _():
        full[0:hp, :] = jnp.zeros((hp, c), F32)

    @pl.when(ti > 0)
    def _():
        full[0:hp, :] = full[tt:tt + hp, :]

    def glu(r0):
        dst = pl.multiple_of(hp + r0, ROW_CHUNK)
        full[pl.ds(dst, ROW_CHUNK), :] = a_ref[pl.ds(r0, ROW_CHUNK), :] * _sigmoid(b_ref[pl.ds(r0, ROW_CHUNK), :])
    _row_loop(tt, ROW_CHUNK, glu)

    rt = 32

    def taps(r0):
        for lb in range(c // lane_blk):
            ls = slice(lb * lane_blk, (lb + 1) * lane_blk)
            win = full[pl.ds(r0, rt + hp), ls]
            acc = jnp.broadcast_to(cb_ref[:, ls], (rt, lane_blk))
            for s in range(8):
                offs = [o for o in range(lead, lead + kw) if o % 8 == s]
                if not offs:
                    continue
                span = offs[-1] - s + rt
                shifted = win if s == 0 else win[s:s + span, :]
                for o in offs:
                    acc = acc + shifted[o - s:o - s + rt, :] * w_ref[o - lead:o - lead + 1, ls]
            cvt[pl.ds(r0, rt), ls] = acc
    _row_loop(tt, rt, taps)

    def ln(r0):
        cv_ref[pl.ds(r0, ROW_CHUNK), :] = _ln_swish(cvt[pl.ds(r0, ROW_CHUNK), :], lg_ref[...], lb_ref[...]).astype(cv_ref.dtype)
    _row_loop(tt, ROW_CHUNK, ln, unroll=2)

    @pl.when(ti == pl.num_programs(1) - 1)
    def _():
        nc_ref[...] = full[tt + lead:tt + hp, :]


def _conv_prompt(proj, nb, t, c, conv_w, conv_b, ln_g, ln_b, tt):
    kw = conv_w.shape[0]
    nt = t // tt
    hp = -(-(kw - 1) // 8) * 8
    vec = pl.BlockSpec((1, c), lambda b, i: (0, 0))
    return pl.pallas_call(
        functools.partial(_conv_prompt_body, kw=kw, lane_blk=LANES),
        grid=(nb, nt),
        in_specs=[pl.BlockSpec((tt, c), lambda b, i: (b * nt + i, 0)),
                  pl.BlockSpec((tt, c), lambda b, i: (b * nt + i, 1)),
                  pl.BlockSpec((kw, c), lambda b, i: (0, 0)), vec, vec, vec],
        out_specs=[pl.BlockSpec((tt, c), lambda b, i: (b * nt + i, 0)),
                   pl.BlockSpec((None, kw - 1, c), lambda b, i: (b, 0, 0))],
        out_shape=[jax.ShapeDtypeStruct((nb * t, c), BF16), jax.ShapeDtypeStruct((nb, kw - 1, c), F32)],
        scratch_shapes=[pltpu.VMEM((tt + hp, c), F32), pltpu.VMEM((tt, c), F32)],
        compiler_params=_cparams(2),
        name="conv_prompt",
    )(proj, proj, conv_w, conv_b, ln_g, ln_b)


def _conv_sample_body(a_ref, b_ref, st_ref, w_ref, cb_ref, lg_ref, lb_ref, cv_ref, nc_ref, glu, cvt, *, t_s, kw):
    db = st_ref.shape[0]
    c = w_ref.shape[1]
    past = kw - 1
    rows = t_s * db

    def make_glu(r0):
        glu[pl.ds(r0, ROW_CHUNK), :] = a_ref[pl.ds(r0, ROW_CHUNK), :] * _sigmoid(b_ref[pl.ds(r0, ROW_CHUNK), :])
    _row_loop(rows, ROW_CHUNK, make_glu)

    def taps(r0):
        for lb in range(c // LANES):
            ls = slice(lb * LANES, (lb + 1) * LANES)
            srcs = []
            for f in range(past + t_s):
                if f < past:
                    srcs.append(st_ref[pl.ds(r0, 8), f * c + ls.start:f * c + ls.stop])
                else:
                    srcs.append(glu[pl.ds(pl.multiple_of((f - past) * db + r0, 8), 8), ls])
            for t in range(t_s):
                acc = jnp.broadcast_to(cb_ref[:, ls], (8, LANES))
                for j in range(kw):
                    acc = acc + srcs[t + j] * w_ref[j:j + 1, ls]
                cvt[pl.ds(pl.multiple_of(t * db + r0, 8), 8), ls] = acc
    _row_loop(db, 8, taps)

    def ln(r0):
        cv_ref[pl.ds(r0, ROW_CHUNK), :] = _ln_swish(cvt[pl.ds(r0, ROW_CHUNK), :], lg_ref[...], lb_ref[...]).astype(cv_ref.dtype)
    _row_loop(rows, ROW_CHUNK, ln)

    for r in range(past):
        f = t_s + r
        if f < past:
            nc_ref[:, r * c:(r + 1) * c] = st_ref[:, f * c:(f + 1) * c]
        else:
            nc_ref[:, r * c:(r + 1) * c] = glu[(f - past) * db:(f - past + 1) * db, :]


def _conv_sample(proj, state2d, db, t_s, c, conv_w, conv_b, ln_g, ln_b):
    kw = conv_w.shape[0]
    rows = t_s * db
    vec = pl.BlockSpec((1, c), lambda i: (0, 0))
    return pl.pallas_call(
        functools.partial(_conv_sample_body, t_s=t_s, kw=kw),
        grid=(1,),
        in_specs=[pl.BlockSpec((rows, c), lambda i: (0, 0)),
                  pl.BlockSpec((rows, c), lambda i: (0, 1)),
                  pl.BlockSpec((db, (kw - 1) * c), lambda i: (0, 0)),
                  pl.BlockSpec((kw, c), lambda i: (0, 0)), vec, vec, vec],
        out_specs=[pl.BlockSpec((rows, c), lambda i: (0, 0)),
                   pl.BlockSpec((db, (kw - 1) * c), lambda i: (0, 0))],
        out_shape=[jax.ShapeDtypeStruct((rows, c), BF16), jax.ShapeDtypeStruct((db, (kw - 1) * c), F32)],
        scratch_shapes=[pltpu.VMEM((rows, c), F32), pltpu.VMEM((rows, c), F32)],
        compiler_params=_cparams(1, 56),
        name="conv_sample",
    )(proj, proj, state2d, conv_w, conv_b, ln_g, ln_b)


def _head_ones(hn):
    ri = lax.broadcasted_iota(jnp.int32, (LANES, LANES), 0)
    ci = lax.broadcasted_iota(jnp.int32, (LANES, LANES), 1)
    return (ri // hn) == (ci // hn)


def _prep_body(*refs, sh, has_state, hn, dlp, alp):
    pr = refs[0:4]
    i0 = 4
    st = None
    if has_state:
        st = refs[i0:i0 + 4]
        i0 += 4
    mu = refs[i0:i0 + 4]
    w0_ref, a0_ref, kk_ref, ka_ref, wl_ref, al_ref, gl_ref = refs[i0 + 4:i0 + 11]
    r_ref, lw_ref, k2_ref, v_ref, a_ref, b_ref, g_ref = refs[i0 + 11:i0 + 18]
    bufs = refs[i0 + 18:i0 + 22]
    lact = refs[i0 + 22]
    tt, w = r_ref.shape
    off = -(-sh // 8) * 8
    ti = pl.program_id(1)

    @pl.when(ti == 0)
    def _():
        for n, bf in enumerate(bufs):
            bf[0:off, :] = st[n][...] if has_state else jnp.zeros((off, bf.shape[1]), F32)

    @pl.when(ti > 0)
    def _():
        for bf in bufs:
            bf[0:off, :] = bf[tt:tt + off, :]

    def copy(r0):
        dst = pl.multiple_of(off + r0, 8)
        for bf, p in zip(bufs, pr):
            bf[pl.ds(dst, 8), :] = p[pl.ds(r0, 8), :]
    _row_loop(tt, 8, copy)

    def shifted(bf, m, r0):
        cur = bf[pl.ds(pl.multiple_of(off + r0, 8), 8), :]
        if sh % 8 == 0:
            prev = bf[pl.ds(pl.multiple_of(off - sh + r0, 8), 8), :]
        else:
            win = bf[pl.ds(r0, 16), :]
            prev = win[8 - sh:16 - sh, :]
        return cur + (prev - cur) * m[...]

    def stage1(r0):
        r_ref[pl.ds(r0, 8), :] = shifted(bufs[0], mu[0], r0)
        k = shifted(bufs[1], mu[1], r0)
        k2_ref[pl.ds(r0, 8), :] = k
        kk = k * kk_ref[...]
        b_ref[pl.ds(r0, 8), :] = kk * kk
        v_ref[pl.ds(r0, 8), :] = shifted(bufs[2], mu[2], r0)
    _row_loop(tt, 8, stage1)

    def stage1l(r0):
        xl = None
        for h in range(2):
            part = shifted(bufs[3], mu[3], r0 + 8 * h)
            xl = part if xl is None else jnp.concatenate([xl, part], axis=0)
        lane = lax.broadcasted_iota(jnp.int32, xl.shape, 1)
        act = jnp.where(lane < dlp, jnp.tanh(xl), jnp.where(lane < dlp + alp, xl, _sigmoid(xl)))
        lact[pl.ds(r0, ROW_CHUNK), :] = act.astype(BF16)
    _row_loop(tt, ROW_CHUNK, stage1l)

    lw_ref[...] = _mm(lact[:, 0:dlp], wl_ref[...])
    a_ref[...] = _mm(lact[:, dlp:dlp + alp], al_ref[...])
    g_ref[...] = _mm(lact[:, dlp + alp:], gl_ref[...])
    ones = jnp.where(_head_ones(hn), 1.0, 0.0).astype(F32)
    for q in range(w // LANES):
        ls = slice(q * LANES, (q + 1) * LANES)
        b_ref[:, ls] = _mm(b_ref[:, ls], ones, HIGHEST)

    def stage3(r0):
        rs = pl.ds(r0, 8)
        k = k2_ref[rs, :]
        lw_ref[rs, :] = -DECAY_SCALE * _sigmoid(w0_ref[...] + lw_ref[rs, :])
        asig = _sigmoid(a0_ref[...] + a_ref[rs, :])
        kk = k * kk_ref[...]
        kkn = kk / jnp.maximum(jnp.sqrt(b_ref[rs, :]), 1e-12)
        k2_ref[rs, :] = k * (1.0 + (asig - 1.0) * ka_ref[...])
        a_ref[rs, :] = -kkn
        b_ref[rs, :] = kkn * asig
    _row_loop(tt, 8, stage3)


def _prep(proj, st, mu_p, w0, a0, k_k, k_a, wl, al, gl, *, n_seq_tiles, tiles_per_seq, tt, c, w, lp, sh, hn,
          out_batch_major=False):
    dlp, alp = wl.shape[0], al.shape[0]
    has_state = st is not None
    rows = n_seq_tiles * tiles_per_seq * tt
    cb = 2 * c // w
    lb = (2 * c + 3 * w) // lp
    off = -(-sh // 8) * 8
    rmap = lambda o: (lambda s, i: (s * tiles_per_seq + i, o))
    in_specs = [pl.BlockSpec((tt, w), rmap(cb)), pl.BlockSpec((tt, w), rmap(cb + 1)),
                pl.BlockSpec((tt, w), rmap(cb + 2)), pl.BlockSpec((tt, lp), rmap(lb))]
    args = [proj, proj, proj, proj]
    if has_state:
        in_specs += [pl.BlockSpec((sh, w), lambda s, i: (0, 0)), pl.BlockSpec((sh, w), lambda s, i: (0, 1)),
                     pl.BlockSpec((sh, w), lambda s, i: (0, 2)), pl.BlockSpec((sh, lp), lambda s, i: (0, 3 * w // lp))]
        args += [st, st, st, st]
    in_specs += [pl.BlockSpec((1, w), lambda s, i: (0, 0)), pl.BlockSpec((1, w), lambda s, i: (0, 1)),
                 pl.BlockSpec((1, w), lambda s, i: (0, 2)), pl.BlockSpec((1, lp), lambda s, i: (0, 3 * w // lp))]
    args += [mu_p, mu_p, mu_p, mu_p]
    vec = pl.BlockSpec((1, w), lambda s, i: (0, 0))
    in_specs += [vec, vec, vec, vec,
                 pl.BlockSpec(wl.shape, lambda s, i: (0, 0)), pl.BlockSpec(al.shape, lambda s, i: (0, 0)),
                 pl.BlockSpec(gl.shape, lambda s, i: (0, 0))]
    args += [w0, a0, k_k, k_a, wl, al, gl]
    if out_batch_major:
        assert n_seq_tiles == 1
        ospec = pl.BlockSpec((tt, w), lambda s, i: (0, i))
        oshape = jax.ShapeDtypeStruct((tt, tiles_per_seq * w), F32)
    else:
        ospec = pl.BlockSpec((tt, w), lambda s, i: (s * tiles_per_seq + i, 0))
        oshape = jax.ShapeDtypeStruct((rows, w), F32)
    return pl.pallas_call(
        functools.partial(_prep_body, sh=sh, has_state=has_state, hn=hn, dlp=dlp, alp=alp),
        grid=(n_seq_tiles, tiles_per_seq),
        in_specs=in_specs,
        out_specs=[ospec] * 7,
        out_shape=[oshape] * 7,
        scratch_shapes=[pltpu.VMEM((off + tt, w), F32)] * 3 + [pltpu.VMEM((off + tt, lp), F32), pltpu.VMEM((tt, lp), BF16)],
        compiler_params=_cparams(2, 56),
        name="rwkv_prep_state" if has_state else "rwkv_prep",
    )(*args)


def _split2(x):
    hi = x.astype(BF16)
    lo = (x - hi.astype(F32)).astype(BF16)
    return hi, lo


def _split3(x):
    hi = x.astype(BF16)
    r1 = x - hi.astype(F32)
    mid = r1.astype(BF16)
    lo = (r1 - mid.astype(F32)).astype(BF16)
    return hi, mid, lo


def _dot3(xs, ys, dims=(((1,), (0,)), ((), ()))):
    def d(a, b):
        return lax.dot_general(a, b, dims, preferred_element_type=F32)
    return d(xs[0], ys[0]) + (d(xs[0], ys[1]) + d(xs[1], ys[0]))


def _dot_exact_rhs(parts, y_bf16):
    out = None
    for p in parts[::-1]:
        d = _mm(p, y_bf16)
        out = d if out is None else out + d
    return out


def _wkv_body(*refs, sub, pp, has_state, hn):
    r_ref, lw_ref, k_ref, v_ref, a_ref, b_ref, g_ref, rk_ref, lg_ref, lb_ref = refs[:10]
    i0 = 10
    s0_ref = None
    if has_state:
        s0_ref = refs[i0]
        i0 += 1
    o_ref, so_ref, sbd = refs[i0:i0 + 3]
    ell = r_ref.shape[0]
    l2 = 2 * ell
    nseq = ell // sub
    n_iter = max(1, int(math.log2(sub)))
    c = pl.program_id(2)
    bd = _head_ones(hn)
    lane = lax.broadcasted_iota(jnp.int32, (1, LANES), 1)
    lo = lane < hn
    row_lo = lax.broadcasted_iota(jnp.int32, (LANES, LANES), 0) < hn

    @pl.when(c == 0)
    def _():
        if has_state:
            for s in range(nseq):
                for q in range(pp):
                    sbd[s * pp + q] = jnp.zeros((LANES, LANES), F32)
                    sbd[s * pp + q, :, 0:hn] = s0_ref[s, q]
                    x = sbd[s * pp + q]
                    sbd[s * pp + q] = jnp.where(bd, x + pltpu.roll(x, hn, 1), 0.0)
        else:
            sbd[...] = jnp.zeros(sbd.shape, F32)

    lw = lw_ref[...]
    ri = lax.broadcasted_iota(jnp.int32, (ell, ell), 0)
    ci = lax.broadcasted_iota(jnp.int32, (ell, ell), 1)
    tri = jnp.where((ri >= ci) & ((ri // sub) == (ci // sub)), 1.0, 0.0).astype(BF16)
    cs = None
    for part in _split3(lw)[::-1]:
        d = _mm(tri, part)
        cs = d if cs is None else cs + d
    w_in = jnp.exp(cs)
    w_inv = jnp.exp(-cs)
    a_t = a_ref[...] * jnp.exp(cs - lw)
    r_t = r_ref[...] * w_in
    b_t = b_ref[...] * w_inv
    k_t = k_ref[...] * w_inv
    rr = lax.broadcasted_iota(jnp.int32, (l2, l2), 0) % ell
    cc = lax.broadcasted_iota(jnp.int32, (l2, l2), 1) % ell
    same_seq = (rr // sub) == (cc // sub)
    strict = (cc < rr) & same_seq
    incl = (cc <= rr) & same_seq
    ones = jnp.where(bd, 1.0, 0.0).astype(BF16)
    avg = jnp.where(bd, 1.0 / hn, 0.0).astype(BF16)

    def seq_rows(x, s):
        return jnp.concatenate([x[s * sub:(s + 1) * sub], x[ell + s * sub:ell + (s + 1) * sub]], axis=0)

    def head_stack(pieces):
        return jnp.concatenate([p[0:sub] for p in pieces] + [p[sub:2 * sub] for p in pieces], axis=0)

    pairs = range(pp)
    lss = [slice(q * LANES, (q + 1) * LANES) for q in pairs]

    def blk(x, q):
        xs = x[:, lss[q]]
        return jnp.concatenate([jnp.where(lo, xs, 0.0), jnp.where(lo, 0.0, xs)], axis=0)

    ab = [blk(a_t, q) for q in pairs]
    rb = [blk(r_t, q) for q in pairs]
    bb = [blk(b_t, q) for q in pairs]
    kb = [blk(k_t, q) for q in pairs]
    vb = [blk(v_ref[...], q) for q in pairs]
    p_s = [_split2(jnp.concatenate([ab[q], rb[q]], axis=0)) for q in pairs]
    bk_s = [_split2(jnp.concatenate([bb[q], kb[q]], axis=0)) for q in pairs]
    v_s = [_split2(vb[q]) for q in pairs]
    g = [_dot3(p_s[q], bk_s[q], _NT) for q in pairs]
    m_s = [_split2(jnp.where(strict, g[q][0:l2, 0:l2], 0.0)) for q in pairs]
    m_ak = [_split2(jnp.where(strict, g[q][0:l2, l2:], 0.0)) for q in pairs]
    m_r = [_split2(jnp.concatenate([jnp.where(incl, g[q][l2:, 0:l2], 0.0), jnp.where(incl, g[q][l2:, l2:], 0.0)], axis=1))
           for q in pairs]

    ps_a, ps_r = [], []
    for q in pairs:
        if nseq == 1:
            ps = _dot3(p_s[q], _split2(sbd[q]), _NT)
            ps_a.append(ps[0:l2])
            ps_r.append(ps[l2:])
        else:
            pa, pr = [], []
            for s in range(nseq):
                sel = jnp.concatenate([seq_rows(ab[q], s), seq_rows(rb[q], s)], axis=0)
                ps = _dot3(_split2(sel), _split2(sbd[s * pp + q]), _NT)
                pa.append(ps[0:2 * sub])
                pr.append(ps[2 * sub:])
            ps_a.append(head_stack(pa))
            ps_r.append(head_stack(pr))

    u = [ps_a[q] + _dot3(m_ak[q], v_s[q]) for q in pairs]
    for it in range(n_iter):
        u = [u[q] + _dot3(m_s[q], _split2(u[q])) for q in pairs]
        if it < n_iter - 1:
            m_s = [_split2(_dot3(m_s[q], m_s[q])) for q in pairs]
    uv = [jnp.concatenate([u[q], vb[q]], axis=0) for q in pairs]
    yb = [ps_r[q] + _dot3(m_r[q], _split2(uv[q])) for q in pairs]
    y = [yb[q][0:ell, :] + yb[q][ell:l2, :] for q in pairs]

    for q in pairs:
        for s in range(nseq):
            if nseq == 1:
                uv_t, bk_sel = uv[q].T, bk_s[q]
            else:
                uv_t = jnp.concatenate([seq_rows(u[q], s), seq_rows(vb[q], s)], axis=0).T
                bk_sel = _split2(jnp.concatenate([seq_rows(bb[q], s), seq_rows(kb[q], s)], axis=0))
            upd = _dot3(_split2(uv_t), bk_sel)
            last = (s + 1) * sub - 1
            sbd[s * pp + q] = (sbd[s * pp + q] + upd) * w_in[last:last + 1, lss[q]]

    mu = [_dot_exact_rhs(_split2(y[q]), avg) for q in pairs]
    d = [y[q] - mu[q] for q in pairs]
    var = [_dot_exact_rhs(_split2(d[q] * d[q]), avg) for q in pairs]
    bonus = [_dot_exact_rhs(_split2(r_ref[:, lss[q]] * k_ref[:, lss[q]] * rk_ref[:, lss[q]]), ones) for q in pairs]
    for q in pairs:
        ls = lss[q]
        yn = d[q] * lax.rsqrt(var[q] + GN_EPS) * lg_ref[:, ls] + lb_ref[:, ls]
        o_ref[:, ls] = ((yn + bonus[q] * v_ref[:, ls]) * g_ref[:, ls]).astype(o_ref.dtype)

    @pl.when(c == pl.num_programs(2) - 1)
    def _():
        for s in range(nseq):
            for q in range(pp):
                x = sbd[s * pp + q]
                so_ref[s, q] = jnp.where(row_lo, x, pltpu.roll(x, hn, 1))[:, 0:hn]


def _wkv_body_highest(*refs, chunk, pp, has_state, hn):
    r_ref, lw_ref, k_ref, v_ref, a_ref, b_ref, g_ref, rk_ref, lg_ref, lb_ref = refs[:10]
    i0 = 10
    s0_ref = None
    if has_state:
        s0_ref = refs[i0]
        i0 += 1
    o_ref, so_ref, sbd = refs[i0:i0 + 3]
    ell = chunk
    l2 = 2 * ell
    n_iter = max(1, int(math.log2(ell)))
    c = pl.program_id(2)
    bd = _head_ones(hn)
    lane = lax.broadcasted_iota(jnp.int32, (1, LANES), 1)
    lo = lane < hn

    @pl.when(c == 0)
    def _():
        if has_state:
            ri = lax.broadcasted_iota(jnp.int32, (hn, LANES), 0)
            ci = lax.broadcasted_iota(jnp.int32, (hn, LANES), 1)
            dup = jnp.where((ci % hn) == ri, 1.0, 0.0).astype(F32)
            for q in range(pp):
                sbd[q] = jnp.where(bd, _mm(s0_ref[q], dup, HIGHEST), 0.0)
        else:
            sbd[...] = jnp.zeros(sbd.shape, F32)

    lw = lw_ref[...]
    ri = lax.broadcasted_iota(jnp.int32, (ell, ell), 0)
    ci = lax.broadcasted_iota(jnp.int32, (ell, ell), 1)
    tri = jnp.where(ri >= ci, 1.0, 0.0).astype(F32)
    cs = _mm(tri, lw, HIGHEST)
    w_in = jnp.exp(cs)
    w_inv = jnp.exp(-cs)
    a_t = a_ref[...] * jnp.exp(cs - lw)
    r_t = r_ref[...] * w_in
    b_t = b_ref[...] * w_inv
    k_t = k_ref[...] * w_inv
    w_last = w_in[ell - 1:ell, :]
    rr = lax.broadcasted_iota(jnp.int32, (l2, l2), 0) % ell
    cc = lax.broadcasted_iota(jnp.int32, (l2, l2), 1) % ell
    strict = cc < rr
    incl = cc <= rr
    ones = jnp.where(bd, 1.0, 0.0).astype(F32)
    avg = jnp.where(bd, 1.0 / hn, 0.0).astype(F32)

    def nt(x, y):
        return lax.dot_general(x, y, _NT, precision=HIGHEST, preferred_element_type=F32)

    for q in range(pp):
        ls = slice(q * LANES, (q + 1) * LANES)

        def blk(x):
            xs = x[:, ls]
            return jnp.concatenate([jnp.where(lo, xs, 0.0), jnp.where(lo, 0.0, xs)], axis=0)

        vq = v_ref[:, ls]
        ab, rb, bb, kb, vb = blk(a_t), blk(r_t), blk(b_t), blk(k_t), blk(v_ref[...])
        s = sbd[q]
        m_ab = jnp.where(strict, nt(ab, bb), 0.0)
        m_ak = jnp.where(strict, nt(ab, kb), 0.0)
        m_rb = jnp.where(incl, nt(rb, bb), 0.0)
        m_rk = jnp.where(incl, nt(rb, kb), 0.0)
        u = nt(ab, s) + _mm(m_ak, vb, HIGHEST)
        for it in range(n_iter):
            u = u + _mm(m_ab, u, HIGHEST)
            if it < n_iter - 1:
                m_ab = _mm(m_ab, m_ab, HIGHEST)
        yb = nt(rb, s) + _mm(m_rb, u, HIGHEST) + _mm(m_rk, vb, HIGHEST)
        y = yb[0:ell, :] + yb[ell:l2, :]
        upd = lax.dot_general(jnp.concatenate([u, vb], axis=0), jnp.concatenate([bb, kb], axis=0), _TN,
                              precision=HIGHEST, preferred_element_type=F32)
        sbd[q] = (s + upd) * w_last[:, ls]

        mu = _mm(y, avg, HIGHEST)
        d = y - mu
        var = _mm(d * d, avg, HIGHEST)
        yn = d * lax.rsqrt(var + GN_EPS) * lg_ref[:, ls] + lb_ref[:, ls]
        bonus = _mm(r_ref[:, ls] * k_ref[:, ls] * rk_ref[:, ls], ones, HIGHEST) * vq
        o_ref[:, ls] = ((yn + bonus) * g_ref[:, ls]).astype(o_ref.dtype)

    @pl.when(c == pl.num_programs(2) - 1)
    def _():
        ri2 = lax.broadcasted_iota(jnp.int32, (LANES, hn), 0)
        ci2 = lax.broadcasted_iota(jnp.int32, (LANES, hn), 1)
        fold = jnp.where((ri2 % hn) == ci2, 1.0, 0.0).astype(F32)
        for q in range(pp):
            so_ref[q] = _mm(sbd[q], fold, HIGHEST)


def _wkv(streams, r_k, ln_g, ln_b, s0, *, nb, t, rows, w, hn, pp, out_dtype):
    pw = pp * LANES
    ng = w // pw
    has_state = s0 is not None
    sub = min(t, rows)
    nseq = rows // sub
    n_chunks = t // sub
    assert n_chunks == 1 or nseq == 1
    sspec = pl.BlockSpec((rows, pw), lambda b, g, c: (b * n_chunks + c, g))
    vec = pl.BlockSpec((1, pw), lambda b, g, c: (0, g))
    stspec = pl.BlockSpec((nseq, pp, LANES, hn), lambda b, g, c: (b, g, 0, 0))
    in_specs = [sspec] * 7 + [vec] * 3 + ([stspec] if has_state else [])
    args = list(streams) + [r_k, ln_g, ln_b] + ([s0] if has_state else [])
    out, s_new = pl.pallas_call(
        functools.partial(_wkv_body, sub=sub, pp=pp, has_state=has_state, hn=hn),
        grid=(nb // nseq, ng, n_chunks),
        in_specs=in_specs,
        out_specs=[sspec, stspec],
        out_shape=[jax.ShapeDtypeStruct((nb * t, w), out_dtype), jax.ShapeDtypeStruct((nb, w // LANES, LANES, hn), F32)],
        scratch_shapes=[pltpu.VMEM((nseq * pp, LANES, LANES), F32)],
        compiler_params=_cparams(3),
        name="wkv_state" if has_state else "wkv",
    )(*args)
    return out, s_new


def _attn_body(q_ref, k_ref, v_ref, o_ref, *scr, nh, scale, cache_kv):
    if cache_kv:
        kb_ref, vb_ref = scr

        @pl.when(pl.program_id(1) == 0)
        def _():
            kb_ref[...] = k_ref[...].astype(BF16)
            vb_ref[...] = v_ref[...].astype(BF16)
    hd = q_ref.shape[1] // nh
    for h in range(nh):
        hs = slice(h * hd, (h + 1) * hd)
        if cache_kv:
            kh, vh = kb_ref[:, hs], vb_ref[:, hs]
        else:
            kh, vh = k_ref[:, hs].astype(BF16), v_ref[:, hs].astype(BF16)
        s = lax.dot_general(q_ref[:, hs].astype(BF16), kh, _NT, preferred_element_type=F32) * scale
        p = jnp.exp(s - jnp.max(s, axis=-1, keepdims=True))
        p = p / jnp.sum(p, axis=-1, keepdims=True)
        o_ref[:, hs] = _mm(p.astype(BF16), vh).astype(o_ref.dtype)


def _attn_prompt(q, k, v, nb, t, nh, tq):
    nm, xd = k.shape[1], k.shape[2]
    nt = t // tq
    kv = pl.BlockSpec((None, nm, xd), lambda b, i: (b, 0, 0))
    return pl.pallas_call(
        functools.partial(_attn_body, nh=nh, scale=(xd // nh) ** -0.5, cache_kv=True),
        grid=(nb, nt),
        in_specs=[pl.BlockSpec((tq, xd), lambda b, i: (b * nt + i, 0)), kv, kv],
        out_specs=pl.BlockSpec((tq, xd), lambda b, i: (b * nt + i, 0)),
        out_shape=jax.ShapeDtypeStruct((nb * t, xd), BF16),
        scratch_shapes=[pltpu.VMEM((nm, xd), BF16)] * 2,
        compiler_params=_cparams(2),
        name="attn_prompt",
    )(q, k, v)


def _attn_sample_body(q_ref, k_ref, v_ref, o_ref, *, scale):
    nm, nh, hd = k_ref.shape
    t_s = q_ref.shape[0]
    k2 = k_ref[...].reshape(nm * nh, hd).astype(BF16)
    v2 = v_ref[...].reshape(nm * nh, hd).astype(BF16)
    q4 = jnp.concatenate([q_ref[:, h * hd:(h + 1) * hd] for h in range(nh)], axis=0).astype(BF16)
    s = lax.dot_general(q4, k2, _NT, preferred_element_type=F32) * scale
    row_head = lax.broadcasted_iota(jnp.int32, s.shape, 0) // t_s
    col_head = lax.broadcasted_iota(jnp.int32, s.shape, 1) % nh
    s = jnp.where(row_head == col_head, s, -1e30)
    p = jnp.exp(s - jnp.max(s, axis=-1, keepdims=True))
    p = p / jnp.sum(p, axis=-1, keepdims=True)
    o4 = _mm(p.astype(BF16), v2)
    for h in range(nh):
        o_ref[:, h * hd:(h + 1) * hd] = o4[h * t_s:(h + 1) * t_s, :]


def _attn_sample(q, k, v, db, t_s):
    _, _, nm, nh, hd = k.shape
    xd = nh * hd
    qs = pl.BlockSpec((t_s, xd), lambda b: (b, 0))
    kv = pl.BlockSpec((None, None, nm, nh, hd), lambda b: (0, b, 0, 0, 0))
    return pl.pallas_call(
        functools.partial(_attn_sample_body, scale=hd ** -0.5),
        grid=(db,),
        in_specs=[qs, kv, kv],
        out_specs=qs,
        out_shape=jax.ShapeDtypeStruct((db * t_s, xd), F32),
        compiler_params=_cparams(1),
        name="attn_sample",
    )(q, k, v)


def _ffn_body(*refs, sh, t_s, has_state, tiles_per_seq):
    x_ref, gpre_ref, wua_ref, wub_ref, cw_ref, cb_ref, wd_ref, gpost_ref = refs[:8]
    i0 = 8
    st = None
    if has_state:
        st = refs[i0:i0 + 4]
        i0 += 4
    o_ref = refs[i0]
    i0 += 1
    n_nf = 4 if has_state else 1
    nf_refs = refs[i0:i0 + n_nf]
    i0 += n_nf
    hn_scr, fa, fb, act = refs[i0:i0 + 4]
    carry = None if has_state else refs[i0 + 4]
    tm, d = x_ref.shape
    tn = fa.shape[1]
    off = max(8, 2 * sh)
    n_t = o_ref.shape[1] // d
    rows = tm // n_t
    i = pl.program_id(0)
    j = pl.program_id(1)

    @pl.when(j == 0)
    def _():
        _norm_rows_to(hn_scr, x_ref, gpre_ref)

    if has_state:
        fa[0:sh, :] = st[0][...]
        fb[0:sh, :] = st[1][...]
        fa[sh:2 * sh, :] = st[2][...]
        fb[sh:2 * sh, :] = st[3][...]
    else:
        first = (i % tiles_per_seq) == 0

        @pl.when(first)
        def _():
            fa[0:off, :] = jnp.zeros((off, tn), F32)
            fb[0:off, :] = jnp.zeros((off, tn), F32)

        @pl.when(jnp.logical_not(first))
        def _():
            fa[0:off, :] = carry[j, :, 0:tn]
            fb[0:off, :] = carry[j, :, tn:2 * tn]

    fa[off:off + tm, :] = _mm(hn_scr[...], wua_ref[...])
    fb[off:off + tm, :] = _mm(hn_scr[...], wub_ref[...])

    if not has_state:
        carry[j, :, 0:tn] = fa[tm:tm + off, :]
        carry[j, :, tn:2 * tn] = fb[tm:tm + off, :]

    def conv_act(r0):
        def conv(f, h):
            if sh % 8 == 0:
                s2 = f[pl.ds(pl.multiple_of(off - 2 * sh + r0, ROW_CHUNK), ROW_CHUNK), :]
                s1 = f[pl.ds(pl.multiple_of(off - sh + r0, ROW_CHUNK), ROW_CHUNK), :]
                s0 = f[pl.ds(pl.multiple_of(off + r0, ROW_CHUNK), ROW_CHUNK), :]
            else:
                win = f[pl.ds(r0, ROW_CHUNK + 8), :]
                s2 = win[8 - 2 * sh:8 - 2 * sh + ROW_CHUNK, :]
                s1 = win[8 - sh:8 - sh + ROW_CHUNK, :]
                s0 = win[8:8 + ROW_CHUNK, :]
            return cb_ref[h:h + 1, :] + s2 * cw_ref[0, h:h + 1, :] + s1 * cw_ref[1, h:h + 1, :] + s0 * cw_ref[2, h:h + 1, :]
        ua = conv(fa, 0)
        ub = conv(fb, 1)
        act[pl.ds(r0, ROW_CHUNK), :] = (ua * _sigmoid(ua) * ub).astype(BF16)
    _row_loop(tm, ROW_CHUNK, conv_act)

    part = _mm(act[...], wd_ref[...])

    @pl.when(j == 0)
    def _():
        for t in range(n_t):
            o_ref[:, t * d:(t + 1) * d] = part[t * rows:(t + 1) * rows, :]

    @pl.when(j > 0)
    def _():
        for t in range(n_t):
            o_ref[:, t * d:(t + 1) * d] += part[t * rows:(t + 1) * rows, :]

    if has_state:
        for r in range(2):
            src = off + (t_s - 2 + r) * sh
            nf_refs[2 * r][...] = fa[src:src + sh, :]
            nf_refs[2 * r + 1][...] = fb[src:src + sh, :]
    else:
        nf = nf_refs[0]
        for r in range(2):
            src = off + tm - 2 + r
            nf[r, 0:1, :] = fa[src:src + 1, :]
            nf[r, 1:2, :] = fb[src:src + 1, :]

    @pl.when(j == pl.num_programs(1) - 1)
    def _():
        g = gpost_ref[...]
        for t in range(n_t):
            def chunk(r0, t=t):
                src = pl.multiple_of(t * rows + r0, ROW_CHUNK)
                y = o_ref[pl.ds(r0, ROW_CHUNK), t * d:(t + 1) * d]
                o_ref[pl.ds(r0, ROW_CHUNK), t * d:(t + 1) * d] = x_ref[pl.ds(src, ROW_CHUNK), :] + _rms(y, g)
            _row_loop(rows, ROW_CHUNK, chunk)


def _ffn_prompt(x, g_pre, w_up, cw3, cb2, w_down, g_post, *, nb, t, tm, tn):
    d = x.shape[1]
    dff = w_down.shape[0]
    nj = dff // tn
    tps = t // tm
    vec = pl.BlockSpec((1, d), lambda i, j: (0, 0))
    out, nf = pl.pallas_call(
        functools.partial(_ffn_body, sh=1, t_s=None, has_state=False, tiles_per_seq=tps),
        grid=(nb * tps, nj),
        in_specs=[pl.BlockSpec((tm, d), lambda i, j: (i, 0)), vec,
                  pl.BlockSpec((d, tn), lambda i, j: (0, j)), pl.BlockSpec((d, tn), lambda i, j: (0, j + nj)),
                  pl.BlockSpec((3, 2, tn), lambda i, j: (0, 0, j)), pl.BlockSpec((2, tn), lambda i, j: (0, j)),
                  pl.BlockSpec((tn, d), lambda i, j: (j, 0)), vec],
        out_specs=[pl.BlockSpec((tm, d), lambda i, j: (i, 0)),
                   pl.BlockSpec((None, 2, 2, tn), lambda i, j: (i, 0, 0, j))],
        out_shape=[jax.ShapeDtypeStruct((nb * t, d), F32), jax.ShapeDtypeStruct((nb * tps, 2, 2, dff), F32)],
        scratch_shapes=[pltpu.VMEM((tm, d), BF16),
                        pltpu.VMEM((8 + tm, tn), F32), pltpu.VMEM((8 + tm, tn), F32),
                        pltpu.VMEM((tm, tn), BF16), pltpu.VMEM((nj, 8, 2 * tn), F32)],
        compiler_params=_cparams(2, 56),
        name="ffn_prompt",
    )(x, g_pre, w_up, w_up, cw3, cb2, w_down, g_post)
    return out, nf.reshape(nb, tps, 2, 2 * dff)[:, -1]


def _ffn_sample(x_tm, st2d, g_pre, w_up, cw3, cb2, w_down, g_post, *, db, t_s, tn):
    d = x_tm.shape[1]
    dff = w_down.shape[0]
    nj = dff // tn
    tm = t_s * db
    vec = pl.BlockSpec((1, d), lambda i, j: (0, 0))
    st_specs = [pl.BlockSpec((db, tn), (lambda i, j, o=o: (0, o * nj + j))) for o in range(4)]
    nf_spec = pl.BlockSpec((db, tn), lambda i, j: (0, j))
    out, n0a, n0b, n1a, n1b = pl.pallas_call(
        functools.partial(_ffn_body, sh=db, t_s=t_s, has_state=True, tiles_per_seq=1),
        grid=(1, nj),
        in_specs=[pl.BlockSpec((tm, d), lambda i, j: (0, 0)), vec,
                  pl.BlockSpec((d, tn), lambda i, j: (0, j)), pl.BlockSpec((d, tn), lambda i, j: (0, j + nj)),
                  pl.BlockSpec((3, 2, tn), lambda i, j: (0, 0, j)), pl.BlockSpec((2, tn), lambda i, j: (0, j)),
                  pl.BlockSpec((tn, d), lambda i, j: (j, 0)), vec, *st_specs],
        out_specs=[pl.BlockSpec((db, t_s * d), lambda i, j: (0, 0)), nf_spec, nf_spec, nf_spec, nf_spec],
        out_shape=[jax.ShapeDtypeStruct((db, t_s * d), F32)] + [jax.ShapeDtypeStruct((db, dff), F32)] * 4,
        scratch_shapes=[pltpu.VMEM((tm, d), BF16),
                        pltpu.VMEM((2 * db + tm, tn), F32), pltpu.VMEM((2 * db + tm, tn), F32),
                        pltpu.VMEM((tm, tn), BF16)],
        compiler_params=_cparams(2, 56),
        name="ffn_sample",
    )(x_tm, g_pre, w_up, w_up, cw3, cb2, w_down, g_post, st2d, st2d, st2d, st2d)
    new_ffn = jnp.stack([jnp.concatenate([n0a, n0b], axis=1), jnp.concatenate([n1a, n1b], axis=1)], axis=1)
    return out, new_ffn


def _pad_cols(x, width):
    return jnp.pad(x, [(0, 0)] * (x.ndim - 1) + [(0, width - x.shape[-1])])


def _round_up(n, m):
    return -(-n // m) * m


def kernel(x_prompt, x_sample, cache_mem_k, cache_mem_v, state_conv, state_shift, state_wkv, state_ffn, mem_prompt, norm_mix_pre, w_in, conv_dw, conv_dw_b, conv_ln_g, conv_ln_b, rwkv_mu, w0, w_lora, a0, a_lora, g_lora, k_k, k_a, r_k, ln_x_g, ln_x_b, w_out, norm_mix_post, norm_xa_pre, norm_mem, w_q, w_k, w_v, w_o, norm_xa_post, norm_ffn_pre, w_up, ffn_dw, ffn_dw_b, w_down, norm_ffn_post):
    nb, t, d = x_prompt.shape
    db, t_s, _ = x_sample.shape
    depth = w_in.shape[0]
    c = conv_dw.shape[-1]
    w = w0.shape[-1]
    n_heads, hn = state_wkv.shape[2], state_wkv.shape[3]
    dl, al, gl = w_lora.shape[1], a_lora.shape[1], g_lora.shape[1]
    dlp, alp, glp = _round_up(dl, LANES), _round_up(al, LANES), _round_up(gl, LANES)
    lp = dlp + alp + glp
    n_mem, xa_heads, xa_hd = cache_mem_k.shape[2:]
    xd = xa_heads * xa_hd
    dff = w_down.shape[1]
    kc = conv_dw.shape[1]
    assert depth == 1 and c == w and (2 * c + 3 * w) % lp == 0 and (3 * w) % lp == 0 and hn * 2 == LANES

    def pad_rcols(x):
        o = 3 * w
        return jnp.concatenate([x[..., :o], _pad_cols(x[..., o:o + dl], dlp), _pad_cols(x[..., o + dl:o + dl + al], alp),
                                _pad_cols(x[..., o + dl + al:], glp)], axis=-1)

    def unpad_rcols(x):
        o = 3 * w
        return jnp.concatenate([x[..., :o + dl], x[..., o + dlp:o + dlp + al], x[..., o + dlp + alp:o + dlp + alp + gl]], axis=-1)

    row = lambda v: v.reshape(1, -1)
    l = 0
    w_in_p = jnp.concatenate([w_in[l][:, :2 * c], pad_rcols(w_in[l][:, 2 * c:])], axis=1).astype(BF16)
    mu_p = pad_rcols(rwkv_mu[l]).reshape(1, -1)
    wl_p = jnp.pad(w_lora[l], ((0, dlp - dl), (0, 0))).astype(BF16)
    al_p = jnp.pad(a_lora[l], ((0, alp - al), (0, 0))).astype(BF16)
    gl_p = jnp.pad(g_lora[l], ((0, glp - gl), (0, 0))).astype(BF16)
    w_out_b, w_q_b, w_k_b, w_v_b, w_o_b = (x[l].astype(BF16) for x in (w_out, w_q, w_k, w_v, w_o))
    w_up_b, w_down_b = w_up[l].astype(BF16), w_down[l].astype(BF16)
    cw3 = ffn_dw[l].reshape(ffn_dw.shape[1], 2, dff)
    cb2 = ffn_dw_b[l].reshape(2, dff)
    g_mix_pre, g_mix_post, g_xa_pre, g_mem, g_xa_post, g_ffn_pre, g_ffn_post = (
        row(x[l]) for x in (norm_mix_pre, norm_mix_post, norm_xa_pre, norm_mem, norm_xa_post, norm_ffn_pre, norm_ffn_post))
    prep_params = (row(w0[l]), row(a0[l]), row(k_k[l]), row(k_a[l]), wl_p, al_p, gl_p)
    wkv_params = (row(r_k[l]), row(ln_x_g[l]), row(ln_x_b[l]))
    conv_params = (conv_dw[l], row(conv_dw_b[l]), row(conv_ln_g[l]), row(conv_ln_b[l]))

    tl = _TILES
    tm_a = min(tl["tm_a"], nb * t)
    tn_a = tl["tn_a"]
    tm_e = min(tl["tm_e"], nb * t)
    tk_e = tl["tk_e"]
    chunk = min(tl["chunk"], t)
    pp = min(tl["pp"], w // LANES)
    tn_f = min(tl["tn_f"], dff)

    n_mem_rows = nb * n_mem
    mem2d = mem_prompt.reshape(n_mem_rows, d)
    mk = _norm_matmul(mem2d, g_mem, w_k_b, tm=min(tl["tm_a"], n_mem_rows), tn=tn_a, out_dtype=F32, name="mem_k")
    mv = _norm_matmul(mem2d, g_mem, w_v_b, tm=min(tl["tm_a"], n_mem_rows), tn=tn_a, out_dtype=F32, name="mem_v")
    xp = x_prompt.reshape(nb * t, d)
    proj = _norm_matmul(xp, g_mix_pre, w_in_p, tm=tm_a, tn=tn_a, out_dtype=F32, name="proj_prompt")
    cv, conv_p = _conv_prompt(proj, nb, t, c, *conv_params, tt=min(tl["tt_conv"], t))
    tt_p = min(tl["tt_prep"], t)
    streams = _prep(proj, None, mu_p, *prep_params, n_seq_tiles=nb, tiles_per_seq=t // tt_p, tt=tt_p, c=c, w=w, lp=lp, sh=1, hn=hn)
    rw, wkv_p = _wkv(streams, *wkv_params, None, nb=nb, t=t, rows=chunk, w=w, hn=hn, pp=pp, out_dtype=BF16)
    x1 = _mm_norm_res([cv, rw], [w_out_b[:c], w_out_b[c:]], g_mix_post, xp, tm=tm_e, tk=tk_e, name="mix_out_prompt")
    q = _norm_matmul(x1, g_xa_pre, w_q_b, tm=tm_a, tn=tn_a, out_dtype=BF16, name="q_prompt")
    o = _attn_prompt(q, mk.reshape(nb, n_mem, xd), mv.reshape(nb, n_mem, xd), nb, t, xa_heads, tq=min(tl["tq"], t))
    x2 = _mm_norm_res([o], [w_o_b], g_xa_post, x1, tm=tm_e, tk=tk_e, name="attn_out_prompt")
    yp, ffn_p = _ffn_prompt(x2, g_ffn_pre, w_up_b, cw3, cb2, w_down_b, g_ffn_post, nb=nb, t=t, tm=min(tl["tm_f"], t), tn=tn_f)
    shift_p = unpad_rcols(proj.reshape(nb, t, -1)[:, -1, 2 * c:])

    rows_s = t_s * db
    xs_tm = jnp.swapaxes(x_sample, 0, 1).reshape(rows_s, d)
    proj_s = _norm_matmul(xs_tm, g_mix_pre, w_in_p, tm=rows_s, tn=tn_a, out_dtype=F32, name="proj_sample")
    cv_s, conv_s = _conv_sample(proj_s, state_conv[l].reshape(db, (kc - 1) * c), db, t_s, c, *conv_params)
    streams_s = _prep(proj_s, pad_rcols(state_shift[l]), mu_p, *prep_params, n_seq_tiles=1, tiles_per_seq=t_s, tt=db,
                      c=c, w=w, lp=lp, sh=db, hn=hn, out_batch_major=True)
    streams_s = [x.reshape(rows_s, w) for x in streams_s]
    rw_s, wkv_s = _wkv(streams_s, *wkv_params, state_wkv[l].reshape(db, w // LANES, LANES, hn), nb=db, t=t_s,
                       rows=tl["chunk"], w=w, hn=hn, pp=pp, out_dtype=F32)
    x1_s = _mm_norm_res([cv_s, rw_s.reshape(db, t_s * w)], [w_out_b[:c], w_out_b[c:]], g_mix_post, xs_tm, tm=db, tk=tk_e,
                        name="mix_out_sample", a_batch_major=(False, True))
    q_s = _norm_matmul(x1_s, g_xa_pre, w_q_b, tm=db, tn=tn_a, out_dtype=F32, name="q_sample", out_batch_major=True)
    o_s = _attn_sample(q_s.reshape(rows_s, xd), cache_mem_k, cache_mem_v, db, t_s)
    x2_s = _mm_norm_res([o_s.reshape(db, t_s * xd)], [w_o_b], g_xa_post, x1_s, tm=db, tk=tk_e, name="attn_out_sample",
                        a_batch_major=(True,))
    ys2, ffn_s = _ffn_sample(x2_s, state_ffn[l].reshape(db, 4 * dff), g_ffn_pre, w_up_b, cw3, cb2, w_down_b, g_ffn_post,
                             db=db, t_s=t_s, tn=tn_f)
    shift_s = unpad_rcols(proj_s[(t_s - 1) * db:, 2 * c:])

    return (yp.reshape(nb, t, d), ys2.reshape(db, t_s, d),
            conv_p[None], conv_s.reshape(db, kc - 1, c)[None],
            shift_p[None], shift_s[None],
            wkv_p.reshape(nb, n_heads, hn, hn)[None], wkv_s.reshape(db, n_heads, hn, hn)[None],
            ffn_p[None], ffn_s[None],
            mk.reshape(nb, n_mem, xa_heads, xa_hd)[None], mv.reshape(nb, n_mem, xa_heads, xa_hd)[None])
```

```python
import functools
import math

import jax
import jax.numpy as jnp
from jax import lax
from jax.experimental import pallas as pl
from jax.experimental.pallas import tpu as pltpu

F32 = jnp.float32
BF16 = jnp.bfloat16
RMS_EPS = 1e-6
LN_EPS = 1e-5
GN_EPS = 64e-5
LANES = 128
ROW_CHUNK = 16
NORM_UNROLL = 4
FFN_SUB = 256
HIGHEST = lax.Precision.HIGHEST
DECAY_SCALE = math.exp(-0.5)
_NT = (((1,), (1,)), ((), ()))
_TN = (((0,), (0,)), ((), ()))


_TILES = dict(
    tm_a=1024, tn_a=512,
    tm_e=512, tk_e=2048,
    tt_conv=256, tt_prep=256,
    chunk=64, pp=8,
    tq=256,
    tm_f=512, tn_f=512,
)


def _cparams(n_grid, vmem_mib=48):
    return pltpu.CompilerParams(dimension_semantics=("arbitrary",) * n_grid,
                                vmem_limit_bytes=vmem_mib * 1024 * 1024)


def _sigmoid(x):
    return 1.0 / (1.0 + jnp.exp(-x))


def _rms(x, g):
    return x * lax.rsqrt(jnp.mean(x * x, axis=-1, keepdims=True) + RMS_EPS) * g


def _row_loop(n_rows, rc, fn, unroll=1):
    def body(i, carry):
        fn(pl.multiple_of(i * rc, rc))
        return carry
    lax.fori_loop(0, n_rows // rc, body, 0, unroll=unroll)


def _res_norm_rows(n_rows, g, load_y, load_res, store):
    n_grp = math.gcd(NORM_UNROLL, n_rows // ROW_CHUNK)

    def group(r0):
        rs = [pl.multiple_of(r0 + k * ROW_CHUNK, ROW_CHUNK) for k in range(n_grp)]
        ys = [load_y(r) for r in rs]
        res = [load_res(r) for r in rs]
        outs = [x + _rms(y, g) for x, y in zip(res, ys)]
        for r, o in zip(rs, outs):
            store(r, o)
    _row_loop(n_rows, ROW_CHUNK * n_grp, group)


def _mm(x, y, precision=None):
    return jnp.dot(x, y, precision=precision, preferred_element_type=F32)


def _norm_rows_to(h_scr, x_ref, g_ref):
    g = g_ref[...]
    rows = x_ref.shape[0]
    d = h_scr.shape[1]
    for t in range(x_ref.shape[1] // d):
        def chunk(r0, t=t):
            dst = pl.multiple_of(t * rows + r0, ROW_CHUNK)
            h_scr[pl.ds(dst, ROW_CHUNK), :] = _rms(x_ref[pl.ds(r0, ROW_CHUNK), t * d:(t + 1) * d], g).astype(BF16)
        _row_loop(rows, ROW_CHUNK, chunk, unroll=NORM_UNROLL)


def _norm_matmul_body(x_ref, g_ref, w_ref, o_ref, h_scr):
    @pl.when(pl.program_id(1) == 0)
    def _():
        _norm_rows_to(h_scr, x_ref, g_ref)

    o_ref[...] = _mm(h_scr[...], w_ref[...]).astype(o_ref.dtype)


def _norm_matmul(x, g, w, *, tm, tn, out_dtype, name, out_batch_major=False):
    d, n = w.shape
    n_t = x.shape[1] // d
    m = x.shape[0] * n_t
    assert n_t == 1 or m == tm
    tn = min(tn, n)
    nj = n // tn
    if out_batch_major:
        out_spec = pl.BlockSpec((tm, tn), lambda i, j: (0, i * nj + j))
        out_shape = jax.ShapeDtypeStruct((tm, (m // tm) * n), out_dtype)
    else:
        out_spec = pl.BlockSpec((tm, tn), lambda i, j: (i, j))
        out_shape = jax.ShapeDtypeStruct((m, n), out_dtype)
    return pl.pallas_call(
        _norm_matmul_body,
        grid=(m // tm, nj),
        in_specs=[pl.BlockSpec((tm // n_t, n_t * d), lambda i, j: (i, 0)),
                  pl.BlockSpec((1, d), lambda i, j: (0, 0)), pl.BlockSpec((d, tn), lambda i, j: (0, j))],
        out_specs=out_spec,
        out_shape=out_shape,
        scratch_shapes=[pltpu.VMEM((tm, d), BF16)],
        compiler_params=_cparams(2),
        name=name,
    )(x, g, w)


def _mm_norm_res_body(*refs, n_a):
    a_refs = refs[:n_a]
    w_refs = refs[n_a:2 * n_a]
    g_ref, r_ref, o_ref = refs[2 * n_a:2 * n_a + 3]
    k = pl.program_id(1)

    part = None
    for a, w in zip(a_refs, w_refs):
        d = _mm(a[...].astype(BF16), w[...])
        part = d if part is None else part + d

    @pl.when(k == 0)
    def _():
        o_ref[...] = part

    @pl.when(k > 0)
    def _():
        o_ref[...] += part

    @pl.when(k == pl.num_programs(1) - 1)
    def _():
        def store(r, v):
            o_ref[pl.ds(r, ROW_CHUNK), :] = v
        _res_norm_rows(o_ref.shape[0], g_ref[...], lambda r: o_ref[pl.ds(r, ROW_CHUNK), :],
                       lambda r: r_ref[pl.ds(r, ROW_CHUNK), :], store)


def _mm_norm_res(a_list, w_list, g, res, *, tm, tk, name, a_batch_major=(), res_batch_major=False):
    kdim, n = w_list[0].shape
    n_a = len(a_list)
    m = res.size // n
    tk = min(tk, kdim)
    nk = kdim // tk
    a_specs = []
    for idx in range(n_a):
        if idx < len(a_batch_major) and a_batch_major[idx]:
            a_specs.append(pl.BlockSpec((tm, tk), lambda i, k: (0, i * nk + k)))
        else:
            a_specs.append(pl.BlockSpec((tm, tk), lambda i, k: (i, k)))
    return pl.pallas_call(
        functools.partial(_mm_norm_res_body, n_a=n_a),
        grid=(m // tm, nk),
        in_specs=[*a_specs,
                  *[pl.BlockSpec((tk, n), lambda i, k: (k, 0)) for _ in w_list],
                  pl.BlockSpec((1, n), lambda i, k: (0, 0)),
                  pl.BlockSpec((tm, n), (lambda i, k: (0, i)) if res_batch_major else (lambda i, k: (i, 0)))],
        out_specs=pl.BlockSpec((tm, n), lambda i, k: (i, 0)),
        out_shape=jax.ShapeDtypeStruct((m, n), F32),
        compiler_params=_cparams(2),
        name=name,
    )(*a_list, *w_list, g, res)


def _ln_swish(y, lg, lb):
    mu = jnp.mean(y, axis=-1, keepdims=True)
    d = y - mu
    var = jnp.mean(d * d, axis=-1, keepdims=True)
    yn = d * lax.rsqrt(var + LN_EPS) * lg + lb
    return yn * _sigmoid(yn)


def _conv_prompt_body(a_ref, b_ref, w_ref, cb_ref, lg_ref, lb_ref, cv_ref, nc_ref, full, cvt, *, kw, lane_blk):
    tt, c = a_ref.shape
    past = kw - 1
    hp = -(-past // 8) * 8
    lead = hp - past
    ti = pl.program_id(1)

    @pl.when(ti == 0)
    def _():
        full[0:hp, :] = jnp.zeros((hp, c), F32)

    @pl.when(ti > 0)
    def _():
        full[0:hp, :] = full[tt:tt + hp, :]

    def glu(r0):
        dst = pl.multiple_of(hp + r0, ROW_CHUNK)
        full[pl.ds(dst, ROW_CHUNK), :] = a_ref[pl.ds(r0, ROW_CHUNK), :] * _sigmoid(b_ref[pl.ds(r0, ROW_CHUNK), :])
    _row_loop(tt, ROW_CHUNK, glu)

    rt = 32

    def taps(r0):
        for lb in range(c // lane_blk):
            ls = slice(lb * lane_blk, (lb + 1) * lane_blk)
            win = full[pl.ds(r0, rt + hp), ls]
            acc = jnp.broadcast_to(cb_ref[:, ls], (rt, lane_blk))
            for s in range(8):
                offs = [o for o in range(lead, lead + kw) if o % 8 == s]
                if not offs:
                    continue
                span = offs[-1] - s + rt
                shifted = win if s == 0 else win[s:s + span, :]
                for o in offs:
                    acc = acc + shifted[o - s:o - s + rt, :] * w_ref[o - lead:o - lead + 1, ls]
            cvt[pl.ds(r0, rt), ls] = acc
    _row_loop(tt, rt, taps)

    def ln(r0):
        cv_ref[pl.ds(r0, ROW_CHUNK), :] = _ln_swish(cvt[pl.ds(r0, ROW_CHUNK), :], lg_ref[...], lb_ref[...]).astype(cv_ref.dtype)
    _row_loop(tt, ROW_CHUNK, ln, unroll=2)

    @pl.when(ti == pl.num_programs(1) - 1)
    def _():
        nc_ref[...] = full[tt + lead:tt + hp, :]


def _conv_prompt(proj, nb, t, c, conv_w, conv_b, ln_g, ln_b, tt):
    kw = conv_w.shape[0]
    nt = t // tt
    hp = -(-(kw - 1) // 8) * 8
    vec = pl.BlockSpec((1, c), lambda b, i: (0, 0))
    return pl.pallas_call(
        functools.partial(_conv_prompt_body, kw=kw, lane_blk=LANES),
        grid=(nb, nt),
        in_specs=[pl.BlockSpec((tt, c), lambda b, i: (b * nt + i, 0)),
                  pl.BlockSpec((tt, c), lambda b, i: (b * nt + i, 1)),
                  pl.BlockSpec((kw, c), lambda b, i: (0, 0)), vec, vec, vec],
        out_specs=[pl.BlockSpec((tt, c), lambda b, i: (b * nt + i, 0)),
                   pl.BlockSpec((None, kw - 1, c), lambda b, i: (b, 0, 0))],
        out_shape=[jax.ShapeDtypeStruct((nb * t, c), BF16), jax.ShapeDtypeStruct((nb, kw - 1, c), F32)],
        scratch_shapes=[pltpu.VMEM((tt + hp, c), F32), pltpu.VMEM((tt, c), F32)],
        compiler_params=_cparams(2),
        name="conv_prompt",
    )(proj, proj, conv_w, conv_b, ln_g, ln_b)


def _conv_sample_body(a_ref, b_ref, st_ref, w_ref, cb_ref, lg_ref, lb_ref, cv_ref, nc_ref, glu, cvt, *, t_s, kw):
    db = st_ref.shape[0]
    c = w_ref.shape[1]
    past = kw - 1
    rows = t_s * db

    def make_glu(r0):
        glu[pl.ds(r0, ROW_CHUNK), :] = a_ref[pl.ds(r0, ROW_CHUNK), :] * _sigmoid(b_ref[pl.ds(r0, ROW_CHUNK), :])
    _row_loop(rows, ROW_CHUNK, make_glu)

    def taps(r0):
        for lb in range(c // LANES):
            ls = slice(lb * LANES, (lb + 1) * LANES)
            srcs = []
            for f in range(past + t_s):
                if f < past:
                    srcs.append(st_ref[pl.ds(r0, 8), f * c + ls.start:f * c + ls.stop])
                else:
                    srcs.append(glu[pl.ds(pl.multiple_of((f - past) * db + r0, 8), 8), ls])
            for t in range(t_s):
                acc = jnp.broadcast_to(cb_ref[:, ls], (8, LANES))
                for j in range(kw):
                    acc = acc + srcs[t + j] * w_ref[j:j + 1, ls]
                cvt[pl.ds(pl.multiple_of(t * db + r0, 8), 8), ls] = acc
    _row_loop(db, 8, taps)

    def ln(r0):
        cv_ref[pl.ds(r0, ROW_CHUNK), :] = _ln_swish(cvt[pl.ds(r0, ROW_CHUNK), :], lg_ref[...], lb_ref[...]).astype(cv_ref.dtype)
    _row_loop(rows, ROW_CHUNK, ln)

    for r in range(past):
        f = t_s + r
        if f < past:
            nc_ref[:, r * c:(r + 1) * c] = st_ref[:, f * c:(f + 1) * c]
        else:
            nc_ref[:, r * c:(r + 1) * c] = glu[(f - past) * db:(f - past + 1) * db, :]


def _conv_sample(proj, state2d, db, t_s, c, conv_w, conv_b, ln_g, ln_b):
    kw = conv_w.shape[0]
    rows = t_s * db
    vec = pl.BlockSpec((1, c), lambda i: (0, 0))
    return pl.pallas_call(
        functools.partial(_conv_sample_body, t_s=t_s, kw=kw),
        grid=(1,),
        in_specs=[pl.BlockSpec((rows, c), lambda i: (0, 0)),
                  pl.BlockSpec((rows, c), lambda i: (0, 1)),
                  pl.BlockSpec((db, (kw - 1) * c), lambda i: (0, 0)),
                  pl.BlockSpec((kw, c), lambda i: (0, 0)), vec, vec, vec],
        out_specs=[pl.BlockSpec((rows, c), lambda i: (0, 0)),
                   pl.BlockSpec((db, (kw - 1) * c), lambda i: (0, 0))],
        out_shape=[jax.ShapeDtypeStruct((rows, c), BF16), jax.ShapeDtypeStruct((db, (kw - 1) * c), F32)],
        scratch_shapes=[pltpu.VMEM((rows, c), F32), pltpu.VMEM((rows, c), F32)],
        compiler_params=_cparams(1, 56),
        name="conv_sample",
    )(proj, proj, state2d, conv_w, conv_b, ln_g, ln_b)


def _head_ones(hn):
    ri = lax.broadcasted_iota(jnp.int32, (LANES, LANES), 0)
    ci = lax.broadcasted_iota(jnp.int32, (LANES, LANES), 1)
    return (ri // hn) == (ci // hn)


def _prep_body(*refs, sh, has_state, hn, dlp, alp):
    pr = refs[0:4]
    i0 = 4
    st = None
    if has_state:
        st = refs[i0:i0 + 4]
        i0 += 4
    mu = refs[i0:i0 + 4]
    w0_ref, a0_ref, kk_ref, ka_ref, wl_ref, al_ref, gl_ref = refs[i0 + 4:i0 + 11]
    r_ref, lw_ref, k2_ref, v_ref, a_ref, b_ref, g_ref = refs[i0 + 11:i0 + 18]
    bufs = refs[i0 + 18:i0 + 22]
    lact = refs[i0 + 22]
    tt, w = r_ref.shape
    off = -(-sh // 8) * 8
    ti = pl.program_id(1)

    @pl.when(ti == 0)
    def _():
        for n, bf in enumerate(bufs):
            bf[0:off, :] = st[n][...] if has_state else jnp.zeros((off, bf.shape[1]), F32)

    @pl.when(ti > 0)
    def _():
        for bf in bufs:
            bf[0:off, :] = bf[tt:tt + off, :]

    def copy(r0):
        dst = pl.multiple_of(off + r0, 8)
        for bf, p in zip(bufs, pr):
            bf[pl.ds(dst, 8), :] = p[pl.ds(r0, 8), :]
    _row_loop(tt, 8, copy)

    def shifted(bf, m, r0):
        cur = bf[pl.ds(pl.multiple_of(off + r0, 8), 8), :]
        if sh % 8 == 0:
            prev = bf[pl.ds(pl.multiple_of(off - sh + r0, 8), 8), :]
        else:
            win = bf[pl.ds(r0, 16), :]
            prev = win[8 - sh:16 - sh, :]
        return cur + (prev - cur) * m[...]

    def stage1(r0):
        r_ref[pl.ds(r0, 8), :] = shifted(bufs[0], mu[0], r0)
        k = shifted(bufs[1], mu[1], r0)
        k2_ref[pl.ds(r0, 8), :] = k
        kk = k * kk_ref[...]
        b_ref[pl.ds(r0, 8), :] = kk * kk
        v_ref[pl.ds(r0, 8), :] = shifted(bufs[2], mu[2], r0)
    _row_loop(tt, 8, stage1)

    def stage1l(r0):
        xl = None
        for h in range(2):
            part = shifted(bufs[3], mu[3], r0 + 8 * h)
            xl = part if xl is None else jnp.concatenate([xl, part], axis=0)
        lane = lax.broadcasted_iota(jnp.int32, xl.shape, 1)
        act = jnp.where(lane < dlp, jnp.tanh(xl), jnp.where(lane < dlp + alp, xl, _sigmoid(xl)))
        lact[pl.ds(r0, ROW_CHUNK), :] = act.astype(BF16)
    _row_loop(tt, ROW_CHUNK, stage1l)

    lw_ref[...] = _mm(lact[:, 0:dlp], wl_ref[...])
    a_ref[...] = _mm(lact[:, dlp:dlp + alp], al_ref[...])
    g_ref[...] = _mm(lact[:, dlp + alp:], gl_ref[...])
    ones = jnp.where(_head_ones(hn), 1.0, 0.0).astype(F32)
    for q in range(w // LANES):
        ls = slice(q * LANES, (q + 1) * LANES)
        b_ref[:, ls] = _mm(b_ref[:, ls], ones, HIGHEST)

    def stage3(r0):
        rs = pl.ds(r0, 8)
        k = k2_ref[rs, :]
        lw_ref[rs, :] = -DECAY_SCALE * _sigmoid(w0_ref[...] + lw_ref[rs, :])
        asig = _sigmoid(a0_ref[...] + a_ref[rs, :])
        kk = k * kk_ref[...]
        kkn = kk / jnp.maximum(jnp.sqrt(b_ref[rs, :]), 1e-12)
        k2_ref[rs, :] = k * (1.0 + (asig - 1.0) * ka_ref[...])
        a_ref[rs, :] = -kkn
        b_ref[rs, :] = kkn * asig
    _row_loop(tt, 8, stage3)


def _prep(proj, st, mu_p, w0, a0, k_k, k_a, wl, al, gl, *, n_seq_tiles, tiles_per_seq, tt, c, w, lp, sh, hn,
          out_batch_major=False):
    dlp, alp = wl.shape[0], al.shape[0]
    has_state = st is not None
    rows = n_seq_tiles * tiles_per_seq * tt
    cb = 2 * c // w
    lb = (2 * c + 3 * w) // lp
    off = -(-sh // 8) * 8
    rmap = lambda o: (lambda s, i: (s * tiles_per_seq + i, o))
    in_specs = [pl.BlockSpec((tt, w), rmap(cb)), pl.BlockSpec((tt, w), rmap(cb + 1)),
                pl.BlockSpec((tt, w), rmap(cb + 2)), pl.BlockSpec((tt, lp), rmap(lb))]
    args = [proj, proj, proj, proj]
    if has_state:
        in_specs += [pl.BlockSpec((sh, w), lambda s, i: (0, 0)), pl.BlockSpec((sh, w), lambda s, i: (0, 1)),
                     pl.BlockSpec((sh, w), lambda s, i: (0, 2)), pl.BlockSpec((sh, lp), lambda s, i: (0, 3 * w // lp))]
        args += [st, st, st, st]
    in_specs += [pl.BlockSpec((1, w), lambda s, i: (0, 0)), pl.BlockSpec((1, w), lambda s, i: (0, 1)),
                 pl.BlockSpec((1, w), lambda s, i: (0, 2)), pl.BlockSpec((1, lp), lambda s, i: (0, 3 * w // lp))]
    args += [mu_p, mu_p, mu_p, mu_p]
    vec = pl.BlockSpec((1, w), lambda s, i: (0, 0))
    in_specs += [vec, vec, vec, vec,
                 pl.BlockSpec(wl.shape, lambda s, i: (0, 0)), pl.BlockSpec(al.shape, lambda s, i: (0, 0)),
                 pl.BlockSpec(gl.shape, lambda s, i: (0, 0))]
    args += [w0, a0, k_k, k_a, wl, al, gl]
    if out_batch_major:
        assert n_seq_tiles == 1
        ospec = pl.BlockSpec((tt, w), lambda s, i: (0, i))
        oshape = jax.ShapeDtypeStruct((tt, tiles_per_seq * w), F32)
    else:
        ospec = pl.BlockSpec((tt, w), lambda s, i: (s * tiles_per_seq + i, 0))
        oshape = jax.ShapeDtypeStruct((rows, w), F32)
    return pl.pallas_call(
        functools.partial(_prep_body, sh=sh, has_state=has_state, hn=hn, dlp=dlp, alp=alp),
        grid=(n_seq_tiles, tiles_per_seq),
        in_specs=in_specs,
        out_specs=[ospec] * 7,
        out_shape=[oshape] * 7,
        scratch_shapes=[pltpu.VMEM((off + tt, w), F32)] * 3 + [pltpu.VMEM((off + tt, lp), F32), pltpu.VMEM((tt, lp), BF16)],
        compiler_params=_cparams(2, 56),
        name="rwkv_prep_state" if has_state else "rwkv_prep",
    )(*args)


def _split2(x):
    hi = x.astype(BF16)
    lo = (x - hi.astype(F32)).astype(BF16)
    return hi, lo


def _split3(x):
    hi = x.astype(BF16)
    r1 = x - hi.astype(F32)
    mid = r1.astype(BF16)
    lo = (r1 - mid.astype(F32)).astype(BF16)
    return hi, mid, lo


def _dot3(xs, ys, dims=(((1,), (0,)), ((), ()))):
    def d(a, b):
        return lax.dot_general(a, b, dims, preferred_element_type=F32)
    return d(xs[0], ys[0]) + (d(xs[0], ys[1]) + d(xs[1], ys[0]))


def _dot_exact_rhs(parts, y_bf16):
    out = None
    for p in parts[::-1]:
        d = _mm(p, y_bf16)
        out = d if out is None else out + d
    return out


def _wkv_body(*refs, sub, pp, has_state, hn):
    r_ref, lw_ref, k_ref, v_ref, a_ref, b_ref, g_ref, rk_ref, lg_ref, lb_ref = refs[:10]
    i0 = 10
    s0_ref = None
    if has_state:
        s0_ref = refs[i0]
        i0 += 1
    o_ref, so_ref, sbd = refs[i0:i0 + 3]
    ell = r_ref.shape[0]
    l2 = 2 * ell
    nseq = ell // sub
    n_iter = max(1, int(math.log2(sub)))
    c = pl.program_id(2)
    bd = _head_ones(hn)
    lane = lax.broadcasted_iota(jnp.int32, (1, LANES), 1)
    lo = lane < hn
    row_lo = lax.broadcasted_iota(jnp.int32, (LANES, LANES), 0) < hn

    @pl.when(c == 0)
    def _():
        if has_state:
            for s in range(nseq):
                for q in range(pp):
                    sbd[s * pp + q] = jnp.zeros((LANES, LANES), F32)
                    sbd[s * pp + q, :, 0:hn] = s0_ref[s, q]
                    x = sbd[s * pp + q]
                    sbd[s * pp + q] = jnp.where(bd, x + pltpu.roll(x, hn, 1), 0.0)
        else:
            sbd[...] = jnp.zeros(sbd.shape, F32)

    lw = lw_ref[...]
    ri = lax.broadcasted_iota(jnp.int32, (ell, ell), 0)
    ci = lax.broadcasted_iota(jnp.int32, (ell, ell), 1)
    tri = jnp.where((ri >= ci) & ((ri // sub) == (ci // sub)), 1.0, 0.0).astype(BF16)
    cs = None
    for part in _split3(lw)[::-1]:
        d = _mm(tri, part)
        cs = d if cs is None else cs + d
    w_in = jnp.exp(cs)
    w_inv = jnp.exp(-cs)
    a_t = a_ref[...] * jnp.exp(cs - lw)
    r_t = r_ref[...] * w_in
    b_t = b_ref[...] * w_inv
    k_t = k_ref[...] * w_inv
    rr = lax.broadcasted_iota(jnp.int32, (l2, l2), 0) % ell
    cc = lax.broadcasted_iota(jnp.int32, (l2, l2), 1) % ell
    same_seq = (rr // sub) == (cc // sub)
    strict = (cc < rr) & same_seq
    incl = (cc <= rr) & same_seq
    ones = jnp.where(bd, 1.0, 0.0).astype(BF16)
    avg = jnp.where(bd, 1.0 / hn, 0.0).astype(BF16)

    def seq_rows(x, s):
        return jnp.concatenate([x[s * sub:(s + 1) * sub], x[ell + s * sub:ell + (s + 1) * sub]], axis=0)

    def head_stack(pieces):
        return jnp.concatenate([p[0:sub] for p in pieces] + [p[sub:2 * sub] for p in pieces], axis=0)

    pairs = range(pp)
    lss = [slice(q * LANES, (q + 1) * LANES) for q in pairs]

    def blk(x, q):
        xs = x[:, lss[q]]
        return jnp.concatenate([jnp.where(lo, xs, 0.0), jnp.where(lo, 0.0, xs)], axis=0)

    ab = [blk(a_t, q) for q in pairs]
    rb = [blk(r_t, q) for q in pairs]
    bb = [blk(b_t, q) for q in pairs]
    kb = [blk(k_t, q) for q in pairs]
    vb = [blk(v_ref[...], q) for q in pairs]
    p_s = [_split2(jnp.concatenate([ab[q], rb[q]], axis=0)) for q in pairs]
    bk_s = [_split2(jnp.concatenate([bb[q], kb[q]], axis=0)) for q in pairs]
    v_s = [_split2(vb[q]) for q in pairs]
    g = [_dot3(p_s[q], bk_s[q], _NT) for q in pairs]
    m_s = [_split2(jnp.where(strict, g[q][0:l2, 0:l2], 0.0)) for q in pairs]
    m_ak = [_split2(jnp.where(strict, g[q][0:l2, l2:], 0.0)) for q in pairs]
    m_r = [_split2(jnp.concatenate([jnp.where(incl, g[q][l2:, 0:l2], 0.0), jnp.where(incl, g[q][l2:, l2:], 0.0)], axis=1))
           for q in pairs]

    ps_a, ps_r = [], []
    for q in pairs:
        if nseq == 1:
            ps = _dot3(p_s[q], _split2(sbd[q]), _NT)
            ps_a.append(ps[0:l2])
            ps_r.append(ps[l2:])
        else:
            pa, pr = [], []
            for s in range(nseq):
                sel = jnp.concatenate([seq_rows(ab[q], s), seq_rows(rb[q], s)], axis=0)
                ps = _dot3(_split2(sel), _split2(sbd[s * pp + q]), _NT)
                pa.append(ps[0:2 * sub])
                pr.append(ps[2 * sub:])
            ps_a.append(head_stack(pa))
            ps_r.append(head_stack(pr))

    u = [ps_a[q] + _dot3(m_ak[q], v_s[q]) for q in pairs]
    for it in range(n_iter):
        u = [u[q] + _dot3(m_s[q], _split2(u[q])) for q in pairs]
        if it < n_iter - 1:
            m_s = [_split2(_dot3(m_s[q], m_s[q])) for q in pairs]
    uv = [jnp.concatenate([u[q], vb[q]], axis=0) for q in pairs]
    yb = [ps_r[q] + _dot3(m_r[q], _split2(uv[q])) for q in pairs]
    y = [yb[q][0:ell, :] + yb[q][ell:l2, :] for q in pairs]

    for q in pairs:
        for s in range(nseq):
            if nseq == 1:
                uv_t, bk_sel = uv[q].T, bk_s[q]
            else:
                uv_t = jnp.concatenate([seq_rows(u[q], s), seq_rows(vb[q], s)], axis=0).T
                bk_sel = _split2(jnp.concatenate([seq_rows(bb[q], s), seq_rows(kb[q], s)], axis=0))
            upd = _dot3(_split2(uv_t), bk_sel)
            last = (s + 1) * sub - 1
            sbd[s * pp + q] = (sbd[s * pp + q] + upd) * w_in[last:last + 1, lss[q]]

    mu = [_dot_exact_rhs(_split2(y[q]), avg) for q in pairs]
    d = [y[q] - mu[q] for q in pairs]
    var = [_dot_exact_rhs(_split2(d[q] * d[q]), avg) for q in pairs]
    bonus = [_dot_exact_rhs(_split2(r_ref[:, lss[q]] * k_ref[:, lss[q]] * rk_ref[:, lss[q]]), ones) for q in pairs]
    for q in pairs:
        ls = lss[q]
        yn = d[q] * lax.rsqrt(var[q] + GN_EPS) * lg_ref[:, ls] + lb_ref[:, ls]
        o_ref[:, ls] = ((yn + bonus[q] * v_ref[:, ls]) * g_ref[:, ls]).astype(o_ref.dtype)

    @pl.when(c == pl.num_programs(2) - 1)
    def _():
        for s in range(nseq):
            for q in range(pp):
                x = sbd[s * pp + q]
                so_ref[s, q] = jnp.where(row_lo, x, pltpu.roll(x, hn, 1))[:, 0:hn]


def _wkv_body_highest(*refs, chunk, pp, has_state, hn):
    r_ref, lw_ref, k_ref, v_ref, a_ref, b_ref, g_ref, rk_ref, lg_ref, lb_ref = refs[:10]
    i0 = 10
    s0_ref = None
    if has_state:
        s0_ref = refs[i0]
        i0 += 1
    o_ref, so_ref, sbd = refs[i0:i0 + 3]
    ell = chunk
    l2 = 2 * ell
    n_iter = max(1, int(math.log2(ell)))
    c = pl.program_id(2)
    bd = _head_ones(hn)
    lane = lax.broadcasted_iota(jnp.int32, (1, LANES), 1)
    lo = lane < hn

    @pl.when(c == 0)
    def _():
        if has_state:
            ri = lax.broadcasted_iota(jnp.int32, (hn, LANES), 0)
            ci = lax.broadcasted_iota(jnp.int32, (hn, LANES), 1)
            dup = jnp.where((ci % hn) == ri, 1.0, 0.0).astype(F32)
            for q in range(pp):
                sbd[q] = jnp.where(bd, _mm(s0_ref[q], dup, HIGHEST), 0.0)
        else:
            sbd[...] = jnp.zeros(sbd.shape, F32)

    lw = lw_ref[...]
    ri = lax.broadcasted_iota(jnp.int32, (ell, ell), 0)
    ci = lax.broadcasted_iota(jnp.int32, (ell, ell), 1)
    tri = jnp.where(ri >= ci, 1.0, 0.0).astype(F32)
    cs = _mm(tri, lw, HIGHEST)
    w_in = jnp.exp(cs)
    w_inv = jnp.exp(-cs)
    a_t = a_ref[...] * jnp.exp(cs - lw)
    r_t = r_ref[...] * w_in
    b_t = b_ref[...] * w_inv
    k_t = k_ref[...] * w_inv
    w_last = w_in[ell - 1:ell, :]
    rr = lax.broadcasted_iota(jnp.int32, (l2, l2), 0) % ell
    cc = lax.broadcasted_iota(jnp.int32, (l2, l2), 1) % ell
    strict = cc < rr
    incl = cc <= rr
    ones = jnp.where(bd, 1.0, 0.0).astype(F32)
    avg = jnp.where(bd, 1.0 / hn, 0.0).astype(F32)

    def nt(x, y):
        return lax.dot_general(x, y, _NT, precision=HIGHEST, preferred_element_type=F32)

    for q in range(pp):
        ls = slice(q * LANES, (q + 1) * LANES)

        def blk(x):
            xs = x[:, ls]
            return jnp.concatenate([jnp.where(lo, xs, 0.0), jnp.where(lo, 0.0, xs)], axis=0)

        vq = v_ref[:, ls]
        ab, rb, bb, kb, vb = blk(a_t), blk(r_t), blk(b_t), blk(k_t), blk(v_ref[...])
        s = sbd[q]
        m_ab = jnp.where(strict, nt(ab, bb), 0.0)
        m_ak = jnp.where(strict, nt(ab, kb), 0.0)
        m_rb = jnp.where(incl, nt(rb, bb), 0.0)
        m_rk = jnp.where(incl, nt(rb, kb), 0.0)
        u = nt(ab, s) + _mm(m_ak, vb, HIGHEST)
        for it in range(n_iter):
            u = u + _mm(m_ab, u, HIGHEST)
            if it < n_iter - 1:
                m_ab = _mm(m_ab, m_ab, HIGHEST)
        yb = nt(rb, s) + _mm(m_rb, u, HIGHEST) + _mm(m_rk, vb, HIGHEST)
        y = yb[0:ell, :] + yb[ell:l2, :]
        upd = lax.dot_general(jnp.concatenate([u, vb], axis=0), jnp.concatenate([bb, kb], axis=0), _TN,
                              precision=HIGHEST, preferred_element_type=F32)
        sbd[q] = (s + upd) * w_last[:, ls]

        mu = _mm(y, avg, HIGHEST)
        d = y - mu
        var = _mm(d * d, avg, HIGHEST)
        yn = d * lax.rsqrt(var + GN_EPS) * lg_ref[:, ls] + lb_ref[:, ls]
        bonus = _mm(r_ref[:, ls] * k_ref[:, ls] * rk_ref[:, ls], ones, HIGHEST) * vq
        o_ref[:, ls] = ((yn + bonus) * g_ref[:, ls]).astype(o_ref.dtype)

    @pl.when(c == pl.num_programs(2) - 1)
    def _():
        ri2 = lax.broadcasted_iota(jnp.int32, (LANES, hn), 0)
        ci2 = lax.broadcasted_iota(jnp.int32, (LANES, hn), 1)
        fold = jnp.where((ri2 % hn) == ci2, 1.0, 0.0).astype(F32)
        for q in range(pp):
            so_ref[q] = _mm(sbd[q], fold, HIGHEST)


def _wkv(streams, r_k, ln_g, ln_b, s0, *, nb, t, rows, w, hn, pp, out_dtype):
    pw = pp * LANES
    ng = w // pw
    has_state = s0 is not None
    sub = min(t, rows)
    nseq = rows // sub
    n_chunks = t // sub
    assert n_chunks == 1 or nseq == 1
    sspec = pl.BlockSpec((rows, pw), lambda b, g, c: (b * n_chunks + c, g))
    vec = pl.BlockSpec((1, pw), lambda b, g, c: (0, g))
    stspec = pl.BlockSpec((nseq, pp, LANES, hn), lambda b, g, c: (b, g, 0, 0))
    in_specs = [sspec] * 7 + [vec] * 3 + ([stspec] if has_state else [])
    args = list(streams) + [r_k, ln_g, ln_b] + ([s0] if has_state else [])
    out, s_new = pl.pallas_call(
        functools.partial(_wkv_body, sub=sub, pp=pp, has_state=has_state, hn=hn),
        grid=(nb // nseq, ng, n_chunks),
        in_specs=in_specs,
        out_specs=[sspec, stspec],
        out_shape=[jax.ShapeDtypeStruct((nb * t, w), out_dtype), jax.ShapeDtypeStruct((nb, w // LANES, LANES, hn), F32)],
        scratch_shapes=[pltpu.VMEM((nseq * pp, LANES, LANES), F32)],
        compiler_params=_cparams(3),
        name="wkv_state" if has_state else "wkv",
    )(*args)
    return out, s_new


def _attn_body(q_ref, k_ref, v_ref, o_ref, *scr, nh, scale, cache_kv):
    if cache_kv:
        kb_ref, vb_ref = scr

        @pl.when(pl.program_id(1) == 0)
        def _():
            kb_ref[...] = k_ref[...].astype(BF16)
            vb_ref[...] = v_ref[...].astype(BF16)
    hd = q_ref.shape[1] // nh
    for h in range(nh):
        hs = slice(h * hd, (h + 1) * hd)
        if cache_kv:
            kh, vh = kb_ref[:, hs], vb_ref[:, hs]
        else:
            kh, vh = k_ref[:, hs].astype(BF16), v_ref[:, hs].astype(BF16)
        s = lax.dot_general(q_ref[:, hs].astype(BF16), kh, _NT, preferred_element_type=F32) * scale
        p = jnp.exp(s - jnp.max(s, axis=-1, keepdims=True))
        p = p / jnp.sum(p, axis=-1, keepdims=True)
        o_ref[:, hs] = _mm(p.astype(BF16), vh).astype(o_ref.dtype)


def _attn_prompt(q, k, v, nb, t, nh, tq):
    nm, xd = k.shape[1], k.shape[2]
    nt = t // tq
    kv = pl.BlockSpec((None, nm, xd), lambda b, i: (b, 0, 0))
    return pl.pallas_call(
        functools.partial(_attn_body, nh=nh, scale=(xd // nh) ** -0.5, cache_kv=True),
        grid=(nb, nt),
        in_specs=[pl.BlockSpec((tq, xd), lambda b, i: (b * nt + i, 0)), kv, kv],
        out_specs=pl.BlockSpec((tq, xd), lambda b, i: (b * nt + i, 0)),
        out_shape=jax.ShapeDtypeStruct((nb * t, xd), BF16),
        scratch_shapes=[pltpu.VMEM((nm, xd), BF16)] * 2,
        compiler_params=_cparams(2),
        name="attn_prompt",
    )(q, k, v)


def _attn_sample_body(q_ref, k_ref, v_ref, o_ref, *, scale):
    nm, nh, hd = k_ref.shape
    t_s = q_ref.shape[0]
    k2 = k_ref[...].reshape(nm * nh, hd).astype(BF16)
    v2 = v_ref[...].reshape(nm * nh, hd).astype(BF16)
    q4 = jnp.concatenate([q_ref[:, h * hd:(h + 1) * hd] for h in range(nh)], axis=0).astype(BF16)
    s = lax.dot_general(q4, k2, _NT, preferred_element_type=F32) * scale
    row_head = lax.broadcasted_iota(jnp.int32, s.shape, 0) // t_s
    col_head = lax.broadcasted_iota(jnp.int32, s.shape, 1) % nh
    s = jnp.where(row_head == col_head, s, -1e30)
    p = jnp.exp(s - jnp.max(s, axis=-1, keepdims=True))
    p = p / jnp.sum(p, axis=-1, keepdims=True)
    o4 = _mm(p.astype(BF16), v2)
    for h in range(nh):
        o_ref[:, h * hd:(h + 1) * hd] = o4[h * t_s:(h + 1) * t_s, :]


def _attn_sample(q, k, v, db, t_s):
    _, _, nm, nh, hd = k.shape
    xd = nh * hd
    qs = pl.BlockSpec((t_s, xd), lambda b: (b, 0))
    kv = pl.BlockSpec((None, None, nm, nh, hd), lambda b: (0, b, 0, 0, 0))
    return pl.pallas_call(
        functools.partial(_attn_sample_body, scale=hd ** -0.5),
        grid=(db,),
        in_specs=[qs, kv, kv],
        out_specs=qs,
        out_shape=jax.ShapeDtypeStruct((db * t_s, xd), F32),
        compiler_params=_cparams(1),
        name="attn_sample",
    )(q, k, v)


def _ffn_body(*refs, sh, t_s, has_state, tiles_per_seq):
    x_ref, gpre_ref, wua_ref, wub_ref, cw_ref, cb_ref, wd_ref, gpost_ref = refs[:8]
    i0 = 8
    st = None
    if has_state:
        st = refs[i0:i0 + 4]
        i0 += 4
    o_ref = refs[i0]
    i0 += 1
    n_nf = 4 if has_state else 1
    nf_refs = refs[i0:i0 + n_nf]
    i0 += n_nf
    hn_scr = refs[i0]
    nsb = (len(refs) - i0 - 1 - (0 if has_state else 1)) // 3
    fas = refs[i0 + 1:i0 + 1 + nsb]
    fbs = refs[i0 + 1 + nsb:i0 + 1 + 2 * nsb]
    acts = refs[i0 + 1 + 2 * nsb:i0 + 1 + 3 * nsb]
    carry = None if has_state else refs[i0 + 1 + 3 * nsb]
    tm, d = x_ref.shape
    sb = fas[0].shape[1]
    tn = nsb * sb
    off = max(8, 2 * sh)
    n_t = o_ref.shape[1] // d
    rows = tm // n_t
    i = pl.program_id(0)
    j = pl.program_id(1)

    @pl.when(j == 0)
    def _():
        _norm_rows_to(hn_scr, x_ref, gpre_ref)
        o_ref[...] = jnp.zeros(o_ref.shape, F32)

    if has_state:
        for s in range(nsb):
            cs = slice(s * sb, (s + 1) * sb)
            fas[s][0:sh, :] = st[0][:, cs]
            fbs[s][0:sh, :] = st[1][:, cs]
            fas[s][sh:2 * sh, :] = st[2][:, cs]
            fbs[s][sh:2 * sh, :] = st[3][:, cs]
    else:
        first = (i % tiles_per_seq) == 0

        @pl.when(first)
        def _():
            for s in range(nsb):
                fas[s][0:off, :] = jnp.zeros((off, sb), F32)
                fbs[s][0:off, :] = jnp.zeros((off, sb), F32)

        @pl.when(jnp.logical_not(first))
        def _():
            for s in range(nsb):
                fas[s][0:off, :] = carry[j, :, s * sb:(s + 1) * sb]
                fbs[s][0:off, :] = carry[j, :, tn + s * sb:tn + (s + 1) * sb]

    def up(s):
        cs = slice(s * sb, (s + 1) * sb)
        fas[s][off:off + tm, :] = _mm(hn_scr[...], wua_ref[:, cs])
        fbs[s][off:off + tm, :] = _mm(hn_scr[...], wub_ref[:, cs])

    def conv_act(s):
        cs = slice(s * sb, (s + 1) * sb)

        def conv(f, h, r0):
            if sh % 8 == 0:
                s2 = f[off - 2 * sh + r0:off - 2 * sh + r0 + ROW_CHUNK, :]
                s1 = f[off - sh + r0:off - sh + r0 + ROW_CHUNK, :]
                s0 = f[off + r0:off + r0 + ROW_CHUNK, :]
            else:
                win = f[r0:r0 + ROW_CHUNK + 8, :]
                s2 = win[8 - 2 * sh:8 - 2 * sh + ROW_CHUNK, :]
                s1 = win[8 - sh:8 - sh + ROW_CHUNK, :]
                s0 = win[8:8 + ROW_CHUNK, :]
            return (cb_ref[h:h + 1, cs] + s2 * cw_ref[0, h:h + 1, cs] + s1 * cw_ref[1, h:h + 1, cs]
                    + s0 * cw_ref[2, h:h + 1, cs])
        for r0 in range(0, tm, ROW_CHUNK):
            ua = conv(fas[s], 0, r0)
            ub = conv(fbs[s], 1, r0)
            acts[s][r0:r0 + ROW_CHUNK, :] = (ua * _sigmoid(ua) * ub).astype(BF16)

    def down(s):
        part = _mm(acts[s][...], wd_ref[s * sb:(s + 1) * sb, :])
        for t in range(n_t):
            o_ref[:, t * d:(t + 1) * d] += part[t * rows:(t + 1) * rows, :]

    up(0)
    for s in range(1, nsb):
        up(s)
        conv_act(s - 1)
        down(s - 1)
    conv_act(nsb - 1)
    down(nsb - 1)

    if not has_state:
        for s in range(nsb):
            carry[j, :, s * sb:(s + 1) * sb] = fas[s][tm:tm + off, :]
            carry[j, :, tn + s * sb:tn + (s + 1) * sb] = fbs[s][tm:tm + off, :]

    for s in range(nsb):
        cs = slice(s * sb, (s + 1) * sb)
        if has_state:
            for r in range(2):
                src = off + (t_s - 2 + r) * sh
                nf_refs[2 * r][:, cs] = fas[s][src:src + sh, :]
                nf_refs[2 * r + 1][:, cs] = fbs[s][src:src + sh, :]
        else:
            nf = nf_refs[0]
            for r in range(2):
                src = off + tm - 2 + r
                nf[r, 0:1, cs] = fas[s][src:src + 1, :]
                nf[r, 1:2, cs] = fbs[s][src:src + 1, :]

    @pl.when(j == pl.num_programs(1) - 1)
    def _():
        for t in range(n_t):
            ts = slice(t * d, (t + 1) * d)

            def store(r, v, ts=ts):
                o_ref[pl.ds(r, ROW_CHUNK), ts] = v
            _res_norm_rows(rows, gpost_ref[...], lambda r, ts=ts: o_ref[pl.ds(r, ROW_CHUNK), ts],
                           lambda r, t=t: x_ref[pl.ds(pl.multiple_of(t * rows + r, ROW_CHUNK), ROW_CHUNK), :], store)


def _ffn_prompt(x, g_pre, w_up, cw3, cb2, w_down, g_post, *, nb, t, tm, tn):
    d = x.shape[1]
    dff = w_down.shape[0]
    nj = dff // tn
    sb = min(FFN_SUB, tn)
    nsb = tn // sb
    tps = t // tm
    vec = pl.BlockSpec((1, d), lambda i, j: (0, 0))
    out, nf = pl.pallas_call(
        functools.partial(_ffn_body, sh=1, t_s=None, has_state=False, tiles_per_seq=tps),
        grid=(nb * tps, nj),
        in_specs=[pl.BlockSpec((tm, d), lambda i, j: (i, 0)), vec,
                  pl.BlockSpec((d, tn), lambda i, j: (0, j)), pl.BlockSpec((d, tn), lambda i, j: (0, j + nj)),
                  pl.BlockSpec((3, 2, tn), lambda i, j: (0, 0, j)), pl.BlockSpec((2, tn), lambda i, j: (0, j)),
                  pl.BlockSpec((tn, d), lambda i, j: (j, 0)), vec],
        out_specs=[pl.BlockSpec((tm, d), lambda i, j: (i, 0)),
                   pl.BlockSpec((None, 2, 2, tn), lambda i, j: (i, 0, 0, j))],
        out_shape=[jax.ShapeDtypeStruct((nb * t, d), F32), jax.ShapeDtypeStruct((nb * tps, 2, 2, dff), F32)],
        scratch_shapes=[pltpu.VMEM((tm, d), BF16),
                        *[pltpu.VMEM((8 + tm, sb), F32)] * (2 * nsb), *[pltpu.VMEM((tm, sb), BF16)] * nsb,
                        pltpu.VMEM((nj, 8, 2 * tn), F32)],
        compiler_params=_cparams(2, 56),
        name="ffn_prompt",
    )(x, g_pre, w_up, w_up, cw3, cb2, w_down, g_post)
    return out, nf.reshape(nb, tps, 2, 2 * dff)[:, -1]


def _ffn_sample(x_tm, st2d, g_pre, w_up, cw3, cb2, w_down, g_post, *, db, t_s, tn):
    d = x_tm.shape[1]
    dff = w_down.shape[0]
    nj = dff // tn
    sb = min(FFN_SUB, tn)
    nsb = tn // sb
    tm = t_s * db
    vec = pl.BlockSpec((1, d), lambda i, j: (0, 0))
    st_specs = [pl.BlockSpec((db, tn), (lambda i, j, o=o: (0, o * nj + j))) for o in range(4)]
    nf_spec = pl.BlockSpec((db, tn), lambda i, j: (0, j))
    out, n0a, n0b, n1a, n1b = pl.pallas_call(
        functools.partial(_ffn_body, sh=db, t_s=t_s, has_state=True, tiles_per_seq=1),
        grid=(1, nj),
        in_specs=[pl.BlockSpec((tm, d), lambda i, j: (0, 0)), vec,
                  pl.BlockSpec((d, tn), lambda i, j: (0, j)), pl.BlockSpec((d, tn), lambda i, j: (0, j + nj)),
                  pl.BlockSpec((3, 2, tn), lambda i, j: (0, 0, j)), pl.BlockSpec((2, tn), lambda i, j: (0, j)),
                  pl.BlockSpec((tn, d), lambda i, j: (j, 0)), vec, *st_specs],
        out_specs=[pl.BlockSpec((db, t_s * d), lambda i, j: (0, 0)), nf_spec, nf_spec, nf_spec, nf_spec],
        out_shape=[jax.ShapeDtypeStruct((db, t_s * d), F32)] + [jax.ShapeDtypeStruct((db, dff), F32)] * 4,
        scratch_shapes=[pltpu.VMEM((tm, d), BF16),
                        *[pltpu.VMEM((2 * db + tm, sb), F32)] * (2 * nsb), *[pltpu.VMEM((tm, sb), BF16)] * nsb],
        compiler_params=_cparams(2, 56),
        name="ffn_sample",
    )(x_tm, g_pre, w_up, w_up, cw3, cb2, w_down, g_post, st2d, st2d, st2d, st2d)
    new_ffn = jnp.stack([jnp.concatenate([n0a, n0b], axis=1), jnp.concatenate([n1a, n1b], axis=1)], axis=1)
    return out, new_ffn


def _pad_cols(x, width):
    return jnp.pad(x, [(0, 0)] * (x.ndim - 1) + [(0, width - x.shape[-1])])


def _round_up(n, m):
    return -(-n // m) * m


def kernel(x_prompt, x_sample, cache_mem_k, cache_mem_v, state_conv, state_shift, state_wkv, state_ffn, mem_prompt, norm_mix_pre, w_in, conv_dw, conv_dw_b, conv_ln_g, conv_ln_b, rwkv_mu, w0, w_lora, a0, a_lora, g_lora, k_k, k_a, r_k, ln_x_g, ln_x_b, w_out, norm_mix_post, norm_xa_pre, norm_mem, w_q, w_k, w_v, w_o, norm_xa_post, norm_ffn_pre, w_up, ffn_dw, ffn_dw_b, w_down, norm_ffn_post):
    nb, t, d = x_prompt.shape
    db, t_s, _ = x_sample.shape
    depth = w_in.shape[0]
    c = conv_dw.shape[-1]
    w = w0.shape[-1]
    n_heads, hn = state_wkv.shape[2], state_wkv.shape[3]
    dl, al, gl = w_lora.shape[1], a_lora.shape[1], g_lora.shape[1]
    dlp, alp, glp = _round_up(dl, LANES), _round_up(al, LANES), _round_up(gl, LANES)
    lp = dlp + alp + glp
    n_mem, xa_heads, xa_hd = cache_mem_k.shape[2:]
    xd = xa_heads * xa_hd
    dff = w_down.shape[1]
    kc = conv_dw.shape[1]
    assert depth == 1 and c == w and (2 * c + 3 * w) % lp == 0 and (3 * w) % lp == 0 and hn * 2 == LANES

    def pad_rcols(x):
        o = 3 * w
        return jnp.concatenate([x[..., :o], _pad_cols(x[..., o:o + dl], dlp), _pad_cols(x[..., o + dl:o + dl + al], alp),
                                _pad_cols(x[..., o + dl + al:], glp)], axis=-1)

    def unpad_rcols(x):
        o = 3 * w
        return jnp.concatenate([x[..., :o + dl], x[..., o + dlp:o + dlp + al], x[..., o + dlp + alp:o + dlp + alp + gl]], axis=-1)

    row = lambda v: v.reshape(1, -1)
    l = 0
    w_in_p = jnp.concatenate([w_in[l][:, :2 * c], pad_rcols(w_in[l][:, 2 * c:])], axis=1).astype(BF16)
    mu_p = pad_rcols(rwkv_mu[l]).reshape(1, -1)
    wl_p = jnp.pad(w_lora[l], ((0, dlp - dl), (0, 0))).astype(BF16)
    al_p = jnp.pad(a_lora[l], ((0, alp - al), (0, 0))).astype(BF16)
    gl_p = jnp.pad(g_lora[l], ((0, glp - gl), (0, 0))).astype(BF16)
    w_out_b, w_q_b, w_k_b, w_v_b, w_o_b = (x[l].astype(BF16) for x in (w_out, w_q, w_k, w_v, w_o))
    w_up_b, w_down_b = w_up[l].astype(BF16), w_down[l].astype(BF16)
    cw3 = ffn_dw[l].reshape(ffn_dw.shape[1], 2, dff)
    cb2 = ffn_dw_b[l].reshape(2, dff)
    g_mix_pre, g_mix_post, g_xa_pre, g_mem, g_xa_post, g_ffn_pre, g_ffn_post = (
        row(x[l]) for x in (norm_mix_pre, norm_mix_post, norm_xa_pre, norm_mem, norm_xa_post, norm_ffn_pre, norm_ffn_post))
    prep_params = (row(w0[l]), row(a0[l]), row(k_k[l]), row(k_a[l]), wl_p, al_p, gl_p)
    wkv_params = (row(r_k[l]), row(ln_x_g[l]), row(ln_x_b[l]))
    conv_params = (conv_dw[l], row(conv_dw_b[l]), row(conv_ln_g[l]), row(conv_ln_b[l]))

    tl = _TILES
    tm_a = min(tl["tm_a"], nb * t)
    tn_a = tl["tn_a"]
    tm_e = min(tl["tm_e"], nb * t)
    tk_e = tl["tk_e"]
    chunk = min(tl["chunk"], t)
    pp = min(tl["pp"], w // LANES)
    tn_f = min(tl["tn_f"], dff)

    n_mem_rows = nb * n_mem
    mem2d = mem_prompt.reshape(n_mem_rows, d)
    mk = _norm_matmul(mem2d, g_mem, w_k_b, tm=min(tl["tm_a"], n_mem_rows), tn=tn_a, out_dtype=F32, name="mem_k")
    mv = _norm_matmul(mem2d, g_mem, w_v_b, tm=min(tl["tm_a"], n_mem_rows), tn=tn_a, out_dtype=F32, name="mem_v")
    xp = x_prompt.reshape(nb * t, d)
    proj = _norm_matmul(xp, g_mix_pre, w_in_p, tm=tm_a, tn=tn_a, out_dtype=F32, name="proj_prompt")
    cv, conv_p = _conv_prompt(proj, nb, t, c, *conv_params, tt=min(tl["tt_conv"], t))
    tt_p = min(tl["tt_prep"], t)
    streams = _prep(proj, None, mu_p, *prep_params, n_seq_tiles=nb, tiles_per_seq=t // tt_p, tt=tt_p, c=c, w=w, lp=lp, sh=1, hn=hn)
    rw, wkv_p = _wkv(streams, *wkv_params, None, nb=nb, t=t, rows=chunk, w=w, hn=hn, pp=pp, out_dtype=BF16)
    x1 = _mm_norm_res([cv, rw], [w_out_b[:c], w_out_b[c:]], g_mix_post, xp, tm=tm_e, tk=tk_e, name="mix_out_prompt")
    q = _norm_matmul(x1, g_xa_pre, w_q_b, tm=tm_a, tn=tn_a, out_dtype=BF16, name="q_prompt")
    o = _attn_prompt(q, mk.reshape(nb, n_mem, xd), mv.reshape(nb, n_mem, xd), nb, t, xa_heads, tq=min(tl["tq"], t))
    x2 = _mm_norm_res([o], [w_o_b], g_xa_post, x1, tm=tm_e, tk=tk_e, name="attn_out_prompt")
    yp, ffn_p = _ffn_prompt(x2, g_ffn_pre, w_up_b, cw3, cb2, w_down_b, g_ffn_post, nb=nb, t=t, tm=min(tl["tm_f"], t), tn=tn_f)
    shift_p = unpad_rcols(proj.reshape(nb, t, -1)[:, -1, 2 * c:])

    rows_s = t_s * db
    xs_bm = x_sample.reshape(db, t_s * d)
    proj_s = _norm_matmul(xs_bm, g_mix_pre, w_in_p, tm=rows_s, tn=tn_a, out_dtype=F32, name="proj_sample")
    cv_s, conv_s = _conv_sample(proj_s, state_conv[l].reshape(db, (kc - 1) * c), db, t_s, c, *conv_params)
    streams_s = _prep(proj_s, pad_rcols(state_shift[l]), mu_p, *prep_params, n_seq_tiles=1, tiles_per_seq=t_s, tt=db,
                      c=c, w=w, lp=lp, sh=db, hn=hn, out_batch_major=True)
    streams_s = [x.reshape(rows_s, w) for x in streams_s]
    rw_s, wkv_s = _wkv(streams_s, *wkv_params, state_wkv[l].reshape(db, w // LANES, LANES, hn), nb=db, t=t_s,
                       rows=tl["chunk"], w=w, hn=hn, pp=pp, out_dtype=F32)
    x1_s = _mm_norm_res([cv_s, rw_s.reshape(db, t_s * w)], [w_out_b[:c], w_out_b[c:]], g_mix_post, xs_bm, tm=db, tk=tk_e,
                        name="mix_out_sample", a_batch_major=(False, True), res_batch_major=True)
    q_s = _norm_matmul(x1_s, g_xa_pre, w_q_b, tm=db, tn=tn_a, out_dtype=F32, name="q_sample", out_batch_major=True)
    o_s = _attn_sample(q_s.reshape(rows_s, xd), cache_mem_k, cache_mem_v, db, t_s)
    x2_s = _mm_norm_res([o_s.reshape(db, t_s * xd)], [w_o_b], g_xa_post, x1_s, tm=db, tk=tk_e, name="attn_out_sample",
                        a_batch_major=(True,))
    ys2, ffn_s = _ffn_sample(x2_s, state_ffn[l].reshape(db, 4 * dff), g_ffn_pre, w_up_b, cw3, cb2, w_down_b, g_ffn_post,
                             db=db, t_s=t_s, tn=tn_f)
    shift_s = unpad_rcols(proj_s[(t_s - 1) * db:, 2 * c:])

    return (yp.reshape(nb, t, d), ys2.reshape(db, t_s, d),
            conv_p[None], conv_s.reshape(db, kc - 1, c)[None],
            shift_p[None], shift_s[None],
            wkv_p.reshape(nb, n_heads, hn, hn)[None], wkv_s.reshape(db, n_heads, hn, hn)[None],
            ffn_p[None], ffn_s[None],
            mk.reshape(nb, n_mem, xa_heads, xa_hd)[None], mv.reshape(nb, n_mem, xa_heads, xa_hd)[None])
```

```python
import functools
import math

import jax
import jax.numpy as jnp
from jax import lax
from jax.experimental import pallas as pl
from jax.experimental.pallas import tpu as pltpu

F32 = jnp.float32
BF16 = jnp.bfloat16
RMS_EPS = 1e-6
LN_EPS = 1e-5
GN_EPS = 64e-5
LANES = 128
ROW_CHUNK = 16
NORM_UNROLL = 4
FFN_SUB = 256
HIGHEST = lax.Precision.HIGHEST
DECAY_SCALE = math.exp(-0.5)
_NT = (((1,), (1,)), ((), ()))


_TILES = dict(
    tm_a=1024, tn_a=512,
    tm_e=512, tk_e=2048,
    tt_conv=256, tt_prep=256,
    chunk=64, pp=8,
    tq=256,
    tm_f=512, tn_f=512,
)


def _cparams(n_grid, vmem_mib=48):
    return pltpu.CompilerParams(dimension_semantics=("arbitrary",) * n_grid,
                                vmem_limit_bytes=vmem_mib * 1024 * 1024)


def _sigmoid(x):
    return 1.0 / (1.0 + jnp.exp(-x))


def _rms(x, g):
    return x * lax.rsqrt(jnp.mean(x * x, axis=-1, keepdims=True) + RMS_EPS) * g


def _row_loop(n_rows, rc, fn, unroll=1):
    def body(i, carry):
        fn(pl.multiple_of(i * rc, rc))
        return carry
    lax.fori_loop(0, n_rows // rc, body, 0, unroll=unroll)


def _res_norm_rows(n_rows, g, load_y, load_res, store):
    n_grp = math.gcd(NORM_UNROLL, n_rows // ROW_CHUNK)

    def group(r0):
        rs = [pl.multiple_of(r0 + k * ROW_CHUNK, ROW_CHUNK) for k in range(n_grp)]
        ys = [load_y(r) for r in rs]
        res = [load_res(r) for r in rs]
        outs = [x + _rms(y, g) for x, y in zip(res, ys)]
        for r, o in zip(rs, outs):
            store(r, o)
    _row_loop(n_rows, ROW_CHUNK * n_grp, group)


def _mm(x, y, precision=None):
    return jnp.dot(x, y, precision=precision, preferred_element_type=F32)


def _norm_rows_to(h_scr, x_ref, g_ref):
    g = g_ref[...]
    rows = x_ref.shape[0]
    d = h_scr.shape[1]
    for t in range(x_ref.shape[1] // d):
        def chunk(r0, t=t):
            dst = pl.multiple_of(t * rows + r0, ROW_CHUNK)
            h_scr[pl.ds(dst, ROW_CHUNK), :] = _rms(x_ref[pl.ds(r0, ROW_CHUNK), t * d:(t + 1) * d], g).astype(BF16)
        _row_loop(rows, ROW_CHUNK, chunk, unroll=NORM_UNROLL)


def _norm_matmul_body(x_ref, g_ref, w_ref, o_ref, h_scr):
    @pl.when(pl.program_id(1) == 0)
    def _():
        _norm_rows_to(h_scr, x_ref, g_ref)

    o_ref[...] = _mm(h_scr[...], w_ref[...]).astype(o_ref.dtype)


def _norm_matmul(x, g, w, *, tm, tn, out_dtype, name, out_batch_major=False):
    d, n = w.shape
    n_t = x.shape[1] // d
    m = x.shape[0] * n_t
    assert n_t == 1 or m == tm
    tn = min(tn, n)
    nj = n // tn
    if out_batch_major:
        out_spec = pl.BlockSpec((tm, tn), lambda i, j: (0, i * nj + j))
        out_shape = jax.ShapeDtypeStruct((tm, (m // tm) * n), out_dtype)
    else:
        out_spec = pl.BlockSpec((tm, tn), lambda i, j: (i, j))
        out_shape = jax.ShapeDtypeStruct((m, n), out_dtype)
    return pl.pallas_call(
        _norm_matmul_body,
        grid=(m // tm, nj),
        in_specs=[pl.BlockSpec((tm // n_t, n_t * d), lambda i, j: (i, 0)),
                  pl.BlockSpec((1, d), lambda i, j: (0, 0)), pl.BlockSpec((d, tn), lambda i, j: (0, j))],
        out_specs=out_spec,
        out_shape=out_shape,
        scratch_shapes=[pltpu.VMEM((tm, d), BF16)],
        compiler_params=_cparams(2),
        name=name,
    )(x, g, w)


def _mm_norm_res_body(*refs, n_a):
    a_refs = refs[:n_a]
    w_refs = refs[n_a:2 * n_a]
    g_ref, r_ref, o_ref = refs[2 * n_a:2 * n_a + 3]
    k = pl.program_id(1)

    part = None
    for a, w in zip(a_refs, w_refs):
        d = _mm(a[...].astype(BF16), w[...])
        part = d if part is None else part + d

    @pl.when(k == 0)
    def _():
        o_ref[...] = part

    @pl.when(k > 0)
    def _():
        o_ref[...] += part

    @pl.when(k == pl.num_programs(1) - 1)
    def _():
        def store(r, v):
            o_ref[pl.ds(r, ROW_CHUNK), :] = v
        _res_norm_rows(o_ref.shape[0], g_ref[...], lambda r: o_ref[pl.ds(r, ROW_CHUNK), :],
                       lambda r: r_ref[pl.ds(r, ROW_CHUNK), :], store)


def _mm_norm_res(a_list, w_list, g, res, *, tm, tk, name, a_batch_major=(), res_batch_major=False):
    kdim, n = w_list[0].shape
    n_a = len(a_list)
    m = res.size // n
    tk = min(tk, kdim)
    nk = kdim // tk
    a_specs = []
    for idx in range(n_a):
        if idx < len(a_batch_major) and a_batch_major[idx]:
            a_specs.append(pl.BlockSpec((tm, tk), lambda i, k: (0, i * nk + k)))
        else:
            a_specs.append(pl.BlockSpec((tm, tk), lambda i, k: (i, k)))
    return pl.pallas_call(
        functools.partial(_mm_norm_res_body, n_a=n_a),
        grid=(m // tm, nk),
        in_specs=[*a_specs,
                  *[pl.BlockSpec((tk, n), lambda i, k: (k, 0)) for _ in w_list],
                  pl.BlockSpec((1, n), lambda i, k: (0, 0)),
                  pl.BlockSpec((tm, n), (lambda i, k: (0, i)) if res_batch_major else (lambda i, k: (i, 0)))],
        out_specs=pl.BlockSpec((tm, n), lambda i, k: (i, 0)),
        out_shape=jax.ShapeDtypeStruct((m, n), F32),
        compiler_params=_cparams(2),
        name=name,
    )(*a_list, *w_list, g, res)


def _ln_swish(y, lg, lb):
    mu = jnp.mean(y, axis=-1, keepdims=True)
    d = y - mu
    var = jnp.mean(d * d, axis=-1, keepdims=True)
    yn = d * lax.rsqrt(var + LN_EPS) * lg + lb
    return yn * _sigmoid(yn)


def _conv_prompt_body(a_ref, b_ref, w_ref, cb_ref, lg_ref, lb_ref, cv_ref, nc_ref, full, cvt, *, kw, lane_blk):
    tt, c = a_ref.shape
    past = kw - 1
    hp = -(-past // 8) * 8
    lead = hp - past
    ti = pl.program_id(1)

    @pl.when(ti == 0)
    def _():
        full[0:hp, :] = jnp.zeros((hp, c), F32)

    @pl.when(ti > 0)
    def _():
        full[0:hp, :] = full[tt:tt + hp, :]

    def glu(r0):
        dst = pl.multiple_of(hp + r0, ROW_CHUNK)
        full[pl.ds(dst, ROW_CHUNK), :] = a_ref[pl.ds(r0, ROW_CHUNK), :] * _sigmoid(b_ref[pl.ds(r0, ROW_CHUNK), :])
    _row_loop(tt, ROW_CHUNK, glu)

    rt = 32

    def taps(r0):
        for lb in range(c // lane_blk):
            ls = slice(lb * lane_blk, (lb + 1) * lane_blk)
            win = full[pl.ds(r0, rt + hp), ls]
            acc = jnp.broadcast_to(cb_ref[:, ls], (rt, lane_blk))
            for s in range(8):
                offs = [o for o in range(lead, lead + kw) if o % 8 == s]
                if not offs:
                    continue
                span = offs[-1] - s + rt
                shifted = win if s == 0 else win[s:s + span, :]
                for o in offs:
                    acc = acc + shifted[o - s:o - s + rt, :] * w_ref[o - lead:o - lead + 1, ls]
            cvt[pl.ds(r0, rt), ls] = acc
    _row_loop(tt, rt, taps)

    def ln(r0):
        cv_ref[pl.ds(r0, ROW_CHUNK), :] = _ln_swish(cvt[pl.ds(r0, ROW_CHUNK), :], lg_ref[...], lb_ref[...]).astype(cv_ref.dtype)
    _row_loop(tt, ROW_CHUNK, ln, unroll=2)

    @pl.when(ti == pl.num_programs(1) - 1)
    def _():
        nc_ref[...] = full[tt + lead:tt + hp, :]


def _conv_prompt(proj, nb, t, c, conv_w, conv_b, ln_g, ln_b, tt):
    kw = conv_w.shape[0]
    nt = t // tt
    hp = -(-(kw - 1) // 8) * 8
    vec = pl.BlockSpec((1, c), lambda b, i: (0, 0))
    return pl.pallas_call(
        functools.partial(_conv_prompt_body, kw=kw, lane_blk=LANES),
        grid=(nb, nt),
        in_specs=[pl.BlockSpec((tt, c), lambda b, i: (b * nt + i, 0)),
                  pl.BlockSpec((tt, c), lambda b, i: (b * nt + i, 1)),
                  pl.BlockSpec((kw, c), lambda b, i: (0, 0)), vec, vec, vec],
        out_specs=[pl.BlockSpec((tt, c), lambda b, i: (b * nt + i, 0)),
                   pl.BlockSpec((None, kw - 1, c), lambda b, i: (b, 0, 0))],
        out_shape=[jax.ShapeDtypeStruct((nb * t, c), BF16), jax.ShapeDtypeStruct((nb, kw - 1, c), F32)],
        scratch_shapes=[pltpu.VMEM((tt + hp, c), F32), pltpu.VMEM((tt, c), F32)],
        compiler_params=_cparams(2),
        name="conv_prompt",
    )(proj, proj, conv_w, conv_b, ln_g, ln_b)


def _conv_sample_body(a_ref, b_ref, st_ref, w_ref, cb_ref, lg_ref, lb_ref, cv_ref, nc_ref, glu, cvt, *, t_s, kw):
    db = st_ref.shape[0]
    c = w_ref.shape[1]
    past = kw - 1
    rows = t_s * db

    def make_glu(r0):
        glu[pl.ds(r0, ROW_CHUNK), :] = a_ref[pl.ds(r0, ROW_CHUNK), :] * _sigmoid(b_ref[pl.ds(r0, ROW_CHUNK), :])
    _row_loop(rows, ROW_CHUNK, make_glu)

    def taps(r0):
        for lb in range(c // LANES):
            ls = slice(lb * LANES, (lb + 1) * LANES)
            srcs = []
            for f in range(past + t_s):
                if f < past:
                    srcs.append(st_ref[pl.ds(r0, 8), f * c + ls.start:f * c + ls.stop])
                else:
                    srcs.append(glu[pl.ds(pl.multiple_of((f - past) * db + r0, 8), 8), ls])
            for t in range(t_s):
                acc = jnp.broadcast_to(cb_ref[:, ls], (8, LANES))
                for j in range(kw):
                    acc = acc + srcs[t + j] * w_ref[j:j + 1, ls]
                cvt[pl.ds(pl.multiple_of(t * db + r0, 8), 8), ls] = acc
    _row_loop(db, 8, taps)

    def ln(r0):
        cv_ref[pl.ds(r0, ROW_CHUNK), :] = _ln_swish(cvt[pl.ds(r0, ROW_CHUNK), :], lg_ref[...], lb_ref[...]).astype(cv_ref.dtype)
    _row_loop(rows, ROW_CHUNK, ln)

    for r in range(past):
        f = t_s + r
        if f < past:
            nc_ref[:, r * c:(r + 1) * c] = st_ref[:, f * c:(f + 1) * c]
        else:
            nc_ref[:, r * c:(r + 1) * c] = glu[(f - past) * db:(f - past + 1) * db, :]


def _conv_sample(proj, state2d, db, t_s, c, conv_w, conv_b, ln_g, ln_b):
    kw = conv_w.shape[0]
    rows = t_s * db
    vec = pl.BlockSpec((1, c), lambda i: (0, 0))
    return pl.pallas_call(
        functools.partial(_conv_sample_body, t_s=t_s, kw=kw),
        grid=(1,),
        in_specs=[pl.BlockSpec((rows, c), lambda i: (0, 0)),
                  pl.BlockSpec((rows, c), lambda i: (0, 1)),
                  pl.BlockSpec((db, (kw - 1) * c), lambda i: (0, 0)),
                  pl.BlockSpec((kw, c), lambda i: (0, 0)), vec, vec, vec],
        out_specs=[pl.BlockSpec((rows, c), lambda i: (0, 0)),
                   pl.BlockSpec((db, (kw - 1) * c), lambda i: (0, 0))],
        out_shape=[jax.ShapeDtypeStruct((rows, c), BF16), jax.ShapeDtypeStruct((db, (kw - 1) * c), F32)],
        scratch_shapes=[pltpu.VMEM((rows, c), F32), pltpu.VMEM((rows, c), F32)],
        compiler_params=_cparams(1, 56),
        name="conv_sample",
    )(proj, proj, state2d, conv_w, conv_b, ln_g, ln_b)


def _head_ones(hn):
    ri = lax.broadcasted_iota(jnp.int32, (LANES, LANES), 0)
    ci = lax.broadcasted_iota(jnp.int32, (LANES, LANES), 1)
    return (ri // hn) == (ci // hn)


def _prep_body(*refs, sh, has_state, hn, dlp, alp):
    pr = refs[0:4]
    i0 = 4
    st = None
    if has_state:
        st = refs[i0:i0 + 4]
        i0 += 4
    mu = refs[i0:i0 + 4]
    w0_ref, a0_ref, kk_ref, ka_ref, wl_ref, al_ref, gl_ref = refs[i0 + 4:i0 + 11]
    r_ref, lw_ref, k2_ref, v_ref, a_ref, b_ref, g_ref = refs[i0 + 11:i0 + 18]
    bufs = refs[i0 + 18:i0 + 22]
    lact = refs[i0 + 22]
    tt, w = r_ref.shape
    off = -(-sh // 8) * 8
    ti = pl.program_id(1)

    @pl.when(ti == 0)
    def _():
        for n, bf in enumerate(bufs):
            bf[0:off, :] = st[n][...] if has_state else jnp.zeros((off, bf.shape[1]), F32)

    @pl.when(ti > 0)
    def _():
        for bf in bufs:
            bf[0:off, :] = bf[tt:tt + off, :]

    def copy(r0):
        dst = pl.multiple_of(off + r0, 8)
        for bf, p in zip(bufs, pr):
            bf[pl.ds(dst, 8), :] = p[pl.ds(r0, 8), :]
    _row_loop(tt, 8, copy)

    def shifted(bf, m, r0):
        cur = bf[pl.ds(pl.multiple_of(off + r0, 8), 8), :]
        if sh % 8 == 0:
            prev = bf[pl.ds(pl.multiple_of(off - sh + r0, 8), 8), :]
        else:
            win = bf[pl.ds(r0, 16), :]
            prev = win[8 - sh:16 - sh, :]
        return cur + (prev - cur) * m[...]

    def stage1(r0):
        r_ref[pl.ds(r0, 8), :] = shifted(bufs[0], mu[0], r0)
        k = shifted(bufs[1], mu[1], r0)
        k2_ref[pl.ds(r0, 8), :] = k
        kk = k * kk_ref[...]
        b_ref[pl.ds(r0, 8), :] = kk * kk
        v_ref[pl.ds(r0, 8), :] = shifted(bufs[2], mu[2], r0)
    _row_loop(tt, 8, stage1)

    def stage1l(r0):
        xl = None
        for h in range(2):
            part = shifted(bufs[3], mu[3], r0 + 8 * h)
            xl = part if xl is None else jnp.concatenate([xl, part], axis=0)
        lane = lax.broadcasted_iota(jnp.int32, xl.shape, 1)
        act = jnp.where(lane < dlp, jnp.tanh(xl), jnp.where(lane < dlp + alp, xl, _sigmoid(xl)))
        lact[pl.ds(r0, ROW_CHUNK), :] = act.astype(BF16)
    _row_loop(tt, ROW_CHUNK, stage1l)

    lw_ref[...] = _mm(lact[:, 0:dlp], wl_ref[...])
    a_ref[...] = _mm(lact[:, dlp:dlp + alp], al_ref[...])
    g_ref[...] = _mm(lact[:, dlp + alp:], gl_ref[...])
    ones = jnp.where(_head_ones(hn), 1.0, 0.0).astype(F32)
    for q in range(w // LANES):
        ls = slice(q * LANES, (q + 1) * LANES)
        b_ref[:, ls] = _mm(b_ref[:, ls], ones, HIGHEST)

    def stage3(r0):
        rs = pl.ds(r0, 8)
        k = k2_ref[rs, :]
        lw_ref[rs, :] = -DECAY_SCALE * _sigmoid(w0_ref[...] + lw_ref[rs, :])
        asig = _sigmoid(a0_ref[...] + a_ref[rs, :])
        kk = k * kk_ref[...]
        kkn = kk / jnp.maximum(jnp.sqrt(b_ref[rs, :]), 1e-12)
        k2_ref[rs, :] = k * (1.0 + (asig - 1.0) * ka_ref[...])
        a_ref[rs, :] = -kkn
        b_ref[rs, :] = kkn * asig
    _row_loop(tt, 8, stage3)


def _prep(proj, st, mu_p, w0, a0, k_k, k_a, wl, al, gl, *, n_seq_tiles, tiles_per_seq, tt, c, w, lp, sh, hn,
          out_batch_major=False):
    dlp, alp = wl.shape[0], al.shape[0]
    has_state = st is not None
    rows = n_seq_tiles * tiles_per_seq * tt
    cb = 2 * c // w
    lb = (2 * c + 3 * w) // lp
    off = -(-sh // 8) * 8
    rmap = lambda o: (lambda s, i: (s * tiles_per_seq + i, o))
    in_specs = [pl.BlockSpec((tt, w), rmap(cb)), pl.BlockSpec((tt, w), rmap(cb + 1)),
                pl.BlockSpec((tt, w), rmap(cb + 2)), pl.BlockSpec((tt, lp), rmap(lb))]
    args = [proj, proj, proj, proj]
    if has_state:
        in_specs += [pl.BlockSpec((sh, w), lambda s, i: (0, 0)), pl.BlockSpec((sh, w), lambda s, i: (0, 1)),
                     pl.BlockSpec((sh, w), lambda s, i: (0, 2)), pl.BlockSpec((sh, lp), lambda s, i: (0, 3 * w // lp))]
        args += [st, st, st, st]
    in_specs += [pl.BlockSpec((1, w), lambda s, i: (0, 0)), pl.BlockSpec((1, w), lambda s, i: (0, 1)),
                 pl.BlockSpec((1, w), lambda s, i: (0, 2)), pl.BlockSpec((1, lp), lambda s, i: (0, 3 * w // lp))]
    args += [mu_p, mu_p, mu_p, mu_p]
    vec = pl.BlockSpec((1, w), lambda s, i: (0, 0))
    in_specs += [vec, vec, vec, vec,
                 pl.BlockSpec(wl.shape, lambda s, i: (0, 0)), pl.BlockSpec(al.shape, lambda s, i: (0, 0)),
                 pl.BlockSpec(gl.shape, lambda s, i: (0, 0))]
    args += [w0, a0, k_k, k_a, wl, al, gl]
    if out_batch_major:
        assert n_seq_tiles == 1
        ospec = pl.BlockSpec((tt, w), lambda s, i: (0, i))
        oshape = jax.ShapeDtypeStruct((tt, tiles_per_seq * w), F32)
    else:
        ospec = pl.BlockSpec((tt, w), lambda s, i: (s * tiles_per_seq + i, 0))
        oshape = jax.ShapeDtypeStruct((rows, w), F32)
    return pl.pallas_call(
        functools.partial(_prep_body, sh=sh, has_state=has_state, hn=hn, dlp=dlp, alp=alp),
        grid=(n_seq_tiles, tiles_per_seq),
        in_specs=in_specs,
        out_specs=[ospec] * 7,
        out_shape=[oshape] * 7,
        scratch_shapes=[pltpu.VMEM((off + tt, w), F32)] * 3 + [pltpu.VMEM((off + tt, lp), F32), pltpu.VMEM((tt, lp), BF16)],
        compiler_params=_cparams(2, 56),
        name="rwkv_prep_state" if has_state else "rwkv_prep",
    )(*args)


def _split2(x):
    hi = x.astype(BF16)
    lo = (x - hi.astype(F32)).astype(BF16)
    return hi, lo


def _split3(x):
    hi = x.astype(BF16)
    r1 = x - hi.astype(F32)
    mid = r1.astype(BF16)
    lo = (r1 - mid.astype(F32)).astype(BF16)
    return hi, mid, lo


def _dot3(xs, ys, dims=(((1,), (0,)), ((), ()))):
    def d(a, b):
        return lax.dot_general(a, b, dims, preferred_element_type=F32)
    return d(xs[0], ys[0]) + (d(xs[0], ys[1]) + d(xs[1], ys[0]))


def _dot_exact_rhs(parts, y_bf16):
    out = None
    for p in parts[::-1]:
        d = _mm(p, y_bf16)
        out = d if out is None else out + d
    return out


def _wkv_body(*refs, sub, pp, has_state, hn):
    r_ref, lw_ref, k_ref, v_ref, a_ref, b_ref, g_ref, rk_ref, lg_ref, lb_ref = refs[:10]
    i0 = 10
    s0_ref = None
    if has_state:
        s0_ref = refs[i0]
        i0 += 1
    o_ref, so_ref, sbd = refs[i0:i0 + 3]
    ell = r_ref.shape[0]
    l2 = 2 * ell
    nseq = ell // sub
    n_iter = max(1, int(math.log2(sub)))
    c = pl.program_id(2)
    bd = _head_ones(hn)
    lane = lax.broadcasted_iota(jnp.int32, (1, LANES), 1)
    lo = lane < hn
    row_lo = lax.broadcasted_iota(jnp.int32, (LANES, LANES), 0) < hn

    @pl.when(c == 0)
    def _():
        if has_state:
            for s in range(nseq):
                for q in range(pp):
                    sbd[s * pp + q] = jnp.zeros((LANES, LANES), F32)
                    sbd[s * pp + q, :, 0:hn] = s0_ref[s, q]
                    x = sbd[s * pp + q]
                    sbd[s * pp + q] = jnp.where(bd, x + pltpu.roll(x, hn, 1), 0.0)
        else:
            sbd[...] = jnp.zeros(sbd.shape, F32)

    lw = lw_ref[...]
    ri = lax.broadcasted_iota(jnp.int32, (ell, ell), 0)
    ci = lax.broadcasted_iota(jnp.int32, (ell, ell), 1)
    tri = jnp.where((ri >= ci) & ((ri // sub) == (ci // sub)), 1.0, 0.0).astype(BF16)
    cs = None
    for part in _split3(lw)[::-1]:
        d = _mm(tri, part)
        cs = d if cs is None else cs + d
    w_in = jnp.exp(cs)
    w_inv = jnp.exp(-cs)
    a_t = a_ref[...] * jnp.exp(cs - lw)
    r_t = r_ref[...] * w_in
    b_t = b_ref[...] * w_inv
    k_t = k_ref[...] * w_inv
    rr = lax.broadcasted_iota(jnp.int32, (l2, l2), 0) % ell
    cc = lax.broadcasted_iota(jnp.int32, (l2, l2), 1) % ell
    same_seq = (rr // sub) == (cc // sub)
    strict = (cc < rr) & same_seq
    incl = (cc <= rr) & same_seq
    ones = jnp.where(bd, 1.0, 0.0).astype(BF16)
    avg = jnp.where(bd, 1.0 / hn, 0.0).astype(BF16)

    def seq_rows(x, s):
        return jnp.concatenate([x[s * sub:(s + 1) * sub], x[ell + s * sub:ell + (s + 1) * sub]], axis=0)

    def head_stack(pieces):
        return jnp.concatenate([p[0:sub] for p in pieces] + [p[sub:2 * sub] for p in pieces], axis=0)

    pairs = range(pp)
    lss = [slice(q * LANES, (q + 1) * LANES) for q in pairs]

    def blk(x, q):
        xs = x[:, lss[q]]
        return jnp.concatenate([jnp.where(lo, xs, 0.0), jnp.where(lo, 0.0, xs)], axis=0)

    ab = [blk(a_t, q) for q in pairs]
    rb = [blk(r_t, q) for q in pairs]
    bb = [blk(b_t, q) for q in pairs]
    kb = [blk(k_t, q) for q in pairs]
    vb = [blk(v_ref[...], q) for q in pairs]
    a_s = [_split2(ab[q]) for q in pairs]
    r_hi = [rb[q].astype(BF16) for q in pairs]
    bk_s = [_split2(jnp.concatenate([bb[q], kb[q]], axis=0)) for q in pairs]
    v_s = [_split2(vb[q]) for q in pairs]
    g_a = [_dot3(a_s[q], bk_s[q], _NT) for q in pairs]
    g_r = [lax.dot_general(r_hi[q], bk_s[q][0], _NT, preferred_element_type=F32) for q in pairs]
    m_s = [_split2(jnp.where(strict, g_a[q][:, 0:l2], 0.0)) for q in pairs]
    m_ak = [_split2(jnp.where(strict, g_a[q][:, l2:], 0.0)) for q in pairs]
    m_r = [jnp.concatenate([jnp.where(incl, g_r[q][:, 0:l2], 0.0), jnp.where(incl, g_r[q][:, l2:], 0.0)],
                           axis=1).astype(BF16) for q in pairs]

    ps_a, ps_r = [], []
    for q in pairs:
        if nseq == 1:
            s_s = _split2(sbd[q])
            ps_a.append(_dot3(a_s[q], s_s, _NT))
            ps_r.append(lax.dot_general(r_hi[q], s_s[0], _NT, preferred_element_type=F32))
        else:
            pa, pr = [], []
            for s in range(nseq):
                sel = jnp.concatenate([seq_rows(ab[q], s), seq_rows(rb[q], s)], axis=0)
                ps = _dot3(_split2(sel), _split2(sbd[s * pp + q]), _NT)
                pa.append(ps[0:2 * sub])
                pr.append(ps[2 * sub:])
            ps_a.append(head_stack(pa))
            ps_r.append(head_stack(pr))

    u = [ps_a[q] + _dot3(m_ak[q], v_s[q]) for q in pairs]
    for it in range(n_iter):
        u = [u[q] + _dot3(m_s[q], _split2(u[q])) for q in pairs]
        if it < n_iter - 1:
            m_s = [_split2(_dot3(m_s[q], m_s[q])) for q in pairs]
    uv = [jnp.concatenate([u[q], vb[q]], axis=0) for q in pairs]
    yb = [ps_r[q] + _mm(m_r[q], uv[q].astype(BF16)) for q in pairs]
    y = [yb[q][0:ell, :] + yb[q][ell:l2, :] for q in pairs]

    for q in pairs:
        for s in range(nseq):
            if nseq == 1:
                uv_t, bk_sel = uv[q].T, bk_s[q]
            else:
                uv_t = jnp.concatenate([seq_rows(u[q], s), seq_rows(vb[q], s)], axis=0).T
                bk_sel = _split2(jnp.concatenate([seq_rows(bb[q], s), seq_rows(kb[q], s)], axis=0))
            upd = _dot3(_split2(uv_t), bk_sel)
            last = (s + 1) * sub - 1
            sbd[s * pp + q] = (sbd[s * pp + q] + upd) * w_in[last:last + 1, lss[q]]

    mu = [_dot_exact_rhs(_split2(y[q]), avg) for q in pairs]
    d = [y[q] - mu[q] for q in pairs]
    var = [_dot_exact_rhs(_split2(d[q] * d[q]), avg) for q in pairs]
    bonus = [_dot_exact_rhs(_split2(r_ref[:, lss[q]] * k_ref[:, lss[q]] * rk_ref[:, lss[q]]), ones) for q in pairs]
    for q in pairs:
        ls = lss[q]
        yn = d[q] * lax.rsqrt(var[q] + GN_EPS) * lg_ref[:, ls] + lb_ref[:, ls]
        o_ref[:, ls] = ((yn + bonus[q] * v_ref[:, ls]) * g_ref[:, ls]).astype(o_ref.dtype)

    @pl.when(c == pl.num_programs(2) - 1)
    def _():
        for s in range(nseq):
            for q in range(pp):
                x = sbd[s * pp + q]
                so_ref[s, q] = jnp.where(row_lo, x, pltpu.roll(x, hn, 1))[:, 0:hn]


def _wkv(streams, r_k, ln_g, ln_b, s0, *, nb, t, rows, w, hn, pp, out_dtype):
    pw = pp * LANES
    ng = w // pw
    has_state = s0 is not None
    sub = min(t, rows)
    nseq = rows // sub
    n_chunks = t // sub
    assert n_chunks == 1 or nseq == 1
    sspec = pl.BlockSpec((rows, pw), lambda b, g, c: (b * n_chunks + c, g))
    vec = pl.BlockSpec((1, pw), lambda b, g, c: (0, g))
    stspec = pl.BlockSpec((nseq, pp, LANES, hn), lambda b, g, c: (b, g, 0, 0))
    in_specs = [sspec] * 7 + [vec] * 3 + ([stspec] if has_state else [])
    args = list(streams) + [r_k, ln_g, ln_b] + ([s0] if has_state else [])
    out, s_new = pl.pallas_call(
        functools.partial(_wkv_body, sub=sub, pp=pp, has_state=has_state, hn=hn),
        grid=(nb // nseq, ng, n_chunks),
        in_specs=in_specs,
        out_specs=[sspec, stspec],
        out_shape=[jax.ShapeDtypeStruct((nb * t, w), out_dtype), jax.ShapeDtypeStruct((nb, w // LANES, LANES, hn), F32)],
        scratch_shapes=[pltpu.VMEM((nseq * pp, LANES, LANES), F32)],
        compiler_params=_cparams(3),
        name="wkv_state" if has_state else "wkv",
    )(*args)
    return out, s_new


def _attn_body(q_ref, k_ref, v_ref, o_ref, kb_ref, vb_ref, *, nh, scale):
    @pl.when(pl.program_id(1) == 0)
    def _():
        kb_ref[...] = k_ref[...].astype(BF16)
        vb_ref[...] = v_ref[...].astype(BF16)
    hd = q_ref.shape[1] // nh
    for h in range(nh):
        hs = slice(h * hd, (h + 1) * hd)
        kh, vh = kb_ref[:, hs], vb_ref[:, hs]
        s = lax.dot_general(q_ref[:, hs].astype(BF16), kh, _NT, preferred_element_type=F32) * scale
        p = jnp.exp(s - jnp.max(s, axis=-1, keepdims=True))
        p = p / jnp.sum(p, axis=-1, keepdims=True)
        o_ref[:, hs] = _mm(p.astype(BF16), vh).astype(o_ref.dtype)


def _attn_prompt(q, k, v, nb, t, nh, tq):
    nm, xd = k.shape[1], k.shape[2]
    nt = t // tq
    kv = pl.BlockSpec((None, nm, xd), lambda b, i: (b, 0, 0))
    return pl.pallas_call(
        functools.partial(_attn_body, nh=nh, scale=(xd // nh) ** -0.5),
        grid=(nb, nt),
        in_specs=[pl.BlockSpec((tq, xd), lambda b, i: (b * nt + i, 0)), kv, kv],
        out_specs=pl.BlockSpec((tq, xd), lambda b, i: (b * nt + i, 0)),
        out_shape=jax.ShapeDtypeStruct((nb * t, xd), BF16),
        scratch_shapes=[pltpu.VMEM((nm, xd), BF16)] * 2,
        compiler_params=_cparams(2),
        name="attn_prompt",
    )(q, k, v)


def _attn_sample_body(q_ref, k_ref, v_ref, o_ref, *, scale):
    nm, nh, hd = k_ref.shape
    t_s = q_ref.shape[0]
    k2 = k_ref[...].reshape(nm * nh, hd).astype(BF16)
    v2 = v_ref[...].reshape(nm * nh, hd).astype(BF16)
    q4 = jnp.concatenate([q_ref[:, h * hd:(h + 1) * hd] for h in range(nh)], axis=0).astype(BF16)
    s = lax.dot_general(q4, k2, _NT, preferred_element_type=F32) * scale
    row_head = lax.broadcasted_iota(jnp.int32, s.shape, 0) // t_s
    col_head = lax.broadcasted_iota(jnp.int32, s.shape, 1) % nh
    s = jnp.where(row_head == col_head, s, -1e30)
    p = jnp.exp(s - jnp.max(s, axis=-1, keepdims=True))
    p = p / jnp.sum(p, axis=-1, keepdims=True)
    o4 = _mm(p.astype(BF16), v2)
    for h in range(nh):
        o_ref[:, h * hd:(h + 1) * hd] = o4[h * t_s:(h + 1) * t_s, :]


def _attn_sample(q, k, v, db, t_s):
    _, _, nm, nh, hd = k.shape
    xd = nh * hd
    qs = pl.BlockSpec((t_s, xd), lambda b: (b, 0))
    kv = pl.BlockSpec((None, None, nm, nh, hd), lambda b: (0, b, 0, 0, 0))
    return pl.pallas_call(
        functools.partial(_attn_sample_body, scale=hd ** -0.5),
        grid=(db,),
        in_specs=[qs, kv, kv],
        out_specs=qs,
        out_shape=jax.ShapeDtypeStruct((db * t_s, xd), F32),
        compiler_params=_cparams(1),
        name="attn_sample",
    )(q, k, v)


def _ffn_body(*refs, sh, t_s, has_state, tiles_per_seq):
    x_ref, gpre_ref, wua_ref, wub_ref, cw_ref, cb_ref, wd_ref, gpost_ref = refs[:8]
    i0 = 8
    st = None
    if has_state:
        st = refs[i0:i0 + 4]
        i0 += 4
    o_ref = refs[i0]
    i0 += 1
    n_nf = 4 if has_state else 1
    nf_refs = refs[i0:i0 + n_nf]
    i0 += n_nf
    hn_scr = refs[i0]
    nsb = (len(refs) - i0 - 1 - (0 if has_state else 1)) // 3
    fas = refs[i0 + 1:i0 + 1 + nsb]
    fbs = refs[i0 + 1 + nsb:i0 + 1 + 2 * nsb]
    acts = refs[i0 + 1 + 2 * nsb:i0 + 1 + 3 * nsb]
    carry = None if has_state else refs[i0 + 1 + 3 * nsb]
    tm, d = x_ref.shape
    sb = fas[0].shape[1]
    tn = nsb * sb
    off = max(8, 2 * sh)
    n_t = o_ref.shape[1] // d
    rows = tm // n_t
    i = pl.program_id(0)
    j = pl.program_id(1)

    @pl.when(j == 0)
    def _():
        _norm_rows_to(hn_scr, x_ref, gpre_ref)
        o_ref[...] = jnp.zeros(o_ref.shape, F32)

    if has_state:
        for s in range(nsb):
            cs = slice(s * sb, (s + 1) * sb)
            fas[s][0:sh, :] = st[0][:, cs]
            fbs[s][0:sh, :] = st[1][:, cs]
            fas[s][sh:2 * sh, :] = st[2][:, cs]
            fbs[s][sh:2 * sh, :] = st[3][:, cs]
    else:
        first = (i % tiles_per_seq) == 0

        @pl.when(first)
        def _():
            for s in range(nsb):
                fas[s][0:off, :] = jnp.zeros((off, sb), F32)
                fbs[s][0:off, :] = jnp.zeros((off, sb), F32)

        @pl.when(jnp.logical_not(first))
        def _():
            for s in range(nsb):
                fas[s][0:off, :] = carry[j, :, s * sb:(s + 1) * sb]
                fbs[s][0:off, :] = carry[j, :, tn + s * sb:tn + (s + 1) * sb]

    def up(s):
        cs = slice(s * sb, (s + 1) * sb)
        fas[s][off:off + tm, :] = _mm(hn_scr[...], wua_ref[:, cs])
        fbs[s][off:off + tm, :] = _mm(hn_scr[...], wub_ref[:, cs])

    def conv_act(s):
        cs = slice(s * sb, (s + 1) * sb)

        def conv(f, h, r0):
            if sh % 8 == 0:
                s2 = f[off - 2 * sh + r0:off - 2 * sh + r0 + ROW_CHUNK, :]
                s1 = f[off - sh + r0:off - sh + r0 + ROW_CHUNK, :]
                s0 = f[off + r0:off + r0 + ROW_CHUNK, :]
            else:
                win = f[r0:r0 + ROW_CHUNK + 8, :]
                s2 = win[8 - 2 * sh:8 - 2 * sh + ROW_CHUNK, :]
                s1 = win[8 - sh:8 - sh + ROW_CHUNK, :]
                s0 = win[8:8 + ROW_CHUNK, :]
            return (cb_ref[h:h + 1, cs] + s2 * cw_ref[0, h:h + 1, cs] + s1 * cw_ref[1, h:h + 1, cs]
                    + s0 * cw_ref[2, h:h + 1, cs])
        for r0 in range(0, tm, ROW_CHUNK):
            ua = conv(fas[s], 0, r0)
            ub = conv(fbs[s], 1, r0)
            acts[s][r0:r0 + ROW_CHUNK, :] = (ua * _sigmoid(ua) * ub).astype(BF16)

    def down(s):
        part = _mm(acts[s][...], wd_ref[s * sb:(s + 1) * sb, :])
        for t in range(n_t):
            o_ref[:, t * d:(t + 1) * d] += part[t * rows:(t + 1) * rows, :]

    up(0)
    for s in range(1, nsb):
        up(s)
        conv_act(s - 1)
        down(s - 1)
    conv_act(nsb - 1)
    down(nsb - 1)

    if not has_state:
        for s in range(nsb):
            carry[j, :, s * sb:(s + 1) * sb] = fas[s][tm:tm + off, :]
            carry[j, :, tn + s * sb:tn + (s + 1) * sb] = fbs[s][tm:tm + off, :]

    for s in range(nsb):
        cs = slice(s * sb, (s + 1) * sb)
        if has_state:
            for r in range(2):
                src = off + (t_s - 2 + r) * sh
                nf_refs[2 * r][:, cs] = fas[s][src:src + sh, :]
                nf_refs[2 * r + 1][:, cs] = fbs[s][src:src + sh, :]
        else:
            nf = nf_refs[0]
            for r in range(2):
                src = off + tm - 2 + r
                nf[r, 0:1, cs] = fas[s][src:src + 1, :]
                nf[r, 1:2, cs] = fbs[s][src:src + 1, :]

    @pl.when(j == pl.num_programs(1) - 1)
    def _():
        for t in range(n_t):
            ts = slice(t * d, (t + 1) * d)

            def store(r, v, ts=ts):
                o_ref[pl.ds(r, ROW_CHUNK), ts] = v
            _res_norm_rows(rows, gpost_ref[...], lambda r, ts=ts: o_ref[pl.ds(r, ROW_CHUNK), ts],
                           lambda r, t=t: x_ref[pl.ds(pl.multiple_of(t * rows + r, ROW_CHUNK), ROW_CHUNK), :], store)


def _ffn_prompt(x, g_pre, w_up, cw3, cb2, w_down, g_post, *, nb, t, tm, tn):
    d = x.shape[1]
    dff = w_down.shape[0]
    nj = dff // tn
    sb = min(FFN_SUB, tn)
    nsb = tn // sb
    tps = t // tm
    vec = pl.BlockSpec((1, d), lambda i, j: (0, 0))
    out, nf = pl.pallas_call(
        functools.partial(_ffn_body, sh=1, t_s=None, has_state=False, tiles_per_seq=tps),
        grid=(nb * tps, nj),
        in_specs=[pl.BlockSpec((tm, d), lambda i, j: (i, 0)), vec,
                  pl.BlockSpec((d, tn), lambda i, j: (0, j)), pl.BlockSpec((d, tn), lambda i, j: (0, j + nj)),
                  pl.BlockSpec((3, 2, tn), lambda i, j: (0, 0, j)), pl.BlockSpec((2, tn), lambda i, j: (0, j)),
                  pl.BlockSpec((tn, d), lambda i, j: (j, 0)), vec],
        out_specs=[pl.BlockSpec((tm, d), lambda i, j: (i, 0)),
                   pl.BlockSpec((None, 2, 2, tn), lambda i, j: (i, 0, 0, j))],
        out_shape=[jax.ShapeDtypeStruct((nb * t, d), F32), jax.ShapeDtypeStruct((nb * tps, 2, 2, dff), F32)],
        scratch_shapes=[pltpu.VMEM((tm, d), BF16),
                        *[pltpu.VMEM((8 + tm, sb), F32)] * (2 * nsb), *[pltpu.VMEM((tm, sb), BF16)] * nsb,
                        pltpu.VMEM((nj, 8, 2 * tn), F32)],
        compiler_params=_cparams(2, 56),
        name="ffn_prompt",
    )(x, g_pre, w_up, w_up, cw3, cb2, w_down, g_post)
    return out, nf.reshape(nb, tps, 2, 2 * dff)[:, -1]


def _ffn_sample(x_tm, st2d, g_pre, w_up, cw3, cb2, w_down, g_post, *, db, t_s, tn):
    d = x_tm.shape[1]
    dff = w_down.shape[0]
    nj = dff // tn
    sb = min(FFN_SUB, tn)
    nsb = tn // sb
    tm = t_s * db
    vec = pl.BlockSpec((1, d), lambda i, j: (0, 0))
    st_specs = [pl.BlockSpec((db, tn), (lambda i, j, o=o: (0, o * nj + j))) for o in range(4)]
    nf_spec = pl.BlockSpec((db, tn), lambda i, j: (0, j))
    out, n0a, n0b, n1a, n1b = pl.pallas_call(
        functools.partial(_ffn_body, sh=db, t_s=t_s, has_state=True, tiles_per_seq=1),
        grid=(1, nj),
        in_specs=[pl.BlockSpec((tm, d), lambda i, j: (0, 0)), vec,
                  pl.BlockSpec((d, tn), lambda i, j: (0, j)), pl.BlockSpec((d, tn), lambda i, j: (0, j + nj)),
                  pl.BlockSpec((3, 2, tn), lambda i, j: (0, 0, j)), pl.BlockSpec((2, tn), lambda i, j: (0, j)),
                  pl.BlockSpec((tn, d), lambda i, j: (j, 0)), vec, *st_specs],
        out_specs=[pl.BlockSpec((db, t_s * d), lambda i, j: (0, 0)), nf_spec, nf_spec, nf_spec, nf_spec],
        out_shape=[jax.ShapeDtypeStruct((db, t_s * d), F32)] + [jax.ShapeDtypeStruct((db, dff), F32)] * 4,
        scratch_shapes=[pltpu.VMEM((tm, d), BF16),
                        *[pltpu.VMEM((2 * db + tm, sb), F32)] * (2 * nsb), *[pltpu.VMEM((tm, sb), BF16)] * nsb],
        compiler_params=_cparams(2, 56),
        name="ffn_sample",
    )(x_tm, g_pre, w_up, w_up, cw3, cb2, w_down, g_post, st2d, st2d, st2d, st2d)
    new_ffn = jnp.stack([jnp.concatenate([n0a, n0b], axis=1), jnp.concatenate([n1a, n1b], axis=1)], axis=1)
    return out, new_ffn


def _pad_cols(x, width):
    return jnp.pad(x, [(0, 0)] * (x.ndim - 1) + [(0, width - x.shape[-1])])


def _round_up(n, m):
    return -(-n // m) * m


def kernel(x_prompt, x_sample, cache_mem_k, cache_mem_v, state_conv, state_shift, state_wkv, state_ffn, mem_prompt, norm_mix_pre, w_in, conv_dw, conv_dw_b, conv_ln_g, conv_ln_b, rwkv_mu, w0, w_lora, a0, a_lora, g_lora, k_k, k_a, r_k, ln_x_g, ln_x_b, w_out, norm_mix_post, norm_xa_pre, norm_mem, w_q, w_k, w_v, w_o, norm_xa_post, norm_ffn_pre, w_up, ffn_dw, ffn_dw_b, w_down, norm_ffn_post):
    nb, t, d = x_prompt.shape
    db, t_s, _ = x_sample.shape
    depth = w_in.shape[0]
    c = conv_dw.shape[-1]
    w = w0.shape[-1]
    n_heads, hn = state_wkv.shape[2], state_wkv.shape[3]
    dl, al, gl = w_lora.shape[1], a_lora.shape[1], g_lora.shape[1]
    dlp, alp, glp = _round_up(dl, LANES), _round_up(al, LANES), _round_up(gl, LANES)
    lp = dlp + alp + glp
    n_mem, xa_heads, xa_hd = cache_mem_k.shape[2:]
    xd = xa_heads * xa_hd
    dff = w_down.shape[1]
    kc = conv_dw.shape[1]
    assert depth == 1 and c == w and (2 * c + 3 * w) % lp == 0 and (3 * w) % lp == 0 and hn * 2 == LANES

    def pad_rcols(x):
        o = 3 * w
        return jnp.concatenate([x[..., :o], _pad_cols(x[..., o:o + dl], dlp), _pad_cols(x[..., o + dl:o + dl + al], alp),
                                _pad_cols(x[..., o + dl + al:], glp)], axis=-1)

    def unpad_rcols(x):
        o = 3 * w
        return jnp.concatenate([x[..., :o + dl], x[..., o + dlp:o + dlp + al], x[..., o + dlp + alp:o + dlp + alp + gl]], axis=-1)

    row = lambda v: v.reshape(1, -1)
    l = 0
    w_in_p = jnp.concatenate([w_in[l][:, :2 * c], pad_rcols(w_in[l][:, 2 * c:])], axis=1).astype(BF16)
    mu_p = pad_rcols(rwkv_mu[l]).reshape(1, -1)
    wl_p = jnp.pad(w_lora[l], ((0, dlp - dl), (0, 0))).astype(BF16)
    al_p = jnp.pad(a_lora[l], ((0, alp - al), (0, 0))).astype(BF16)
    gl_p = jnp.pad(g_lora[l], ((0, glp - gl), (0, 0))).astype(BF16)
    w_out_b, w_q_b, w_k_b, w_v_b, w_o_b = (x[l].astype(BF16) for x in (w_out, w_q, w_k, w_v, w_o))
    w_up_b, w_down_b = w_up[l].astype(BF16), w_down[l].astype(BF16)
    cw3 = ffn_dw[l].reshape(ffn_dw.shape[1], 2, dff)
    cb2 = ffn_dw_b[l].reshape(2, dff)
    g_mix_pre, g_mix_post, g_xa_pre, g_mem, g_xa_post, g_ffn_pre, g_ffn_post = (
        row(x[l]) for x in (norm_mix_pre, norm_mix_post, norm_xa_pre, norm_mem, norm_xa_post, norm_ffn_pre, norm_ffn_post))
    prep_params = (row(w0[l]), row(a0[l]), row(k_k[l]), row(k_a[l]), wl_p, al_p, gl_p)
    wkv_params = (row(r_k[l]), row(ln_x_g[l]), row(ln_x_b[l]))
    conv_params = (conv_dw[l], row(conv_dw_b[l]), row(conv_ln_g[l]), row(conv_ln_b[l]))

    tl = _TILES
    tm_a = min(tl["tm_a"], nb * t)
    tn_a = tl["tn_a"]
    tm_e = min(tl["tm_e"], nb * t)
    tk_e = tl["tk_e"]
    chunk = min(tl["chunk"], t)
    pp = min(tl["pp"], w // LANES)
    tn_f = min(tl["tn_f"], dff)

    n_mem_rows = nb * n_mem
    mem2d = mem_prompt.reshape(n_mem_rows, d)
    mk = _norm_matmul(mem2d, g_mem, w_k_b, tm=min(tl["tm_a"], n_mem_rows), tn=tn_a, out_dtype=F32, name="mem_k")
    mv = _norm_matmul(mem2d, g_mem, w_v_b, tm=min(tl["tm_a"], n_mem_rows), tn=tn_a, out_dtype=F32, name="mem_v")
    xp = x_prompt.reshape(nb * t, d)
    proj = _norm_matmul(xp, g_mix_pre, w_in_p, tm=tm_a, tn=tn_a, out_dtype=F32, name="proj_prompt")
    cv, conv_p = _conv_prompt(proj, nb, t, c, *conv_params, tt=min(tl["tt_conv"], t))
    tt_p = min(tl["tt_prep"], t)
    streams = _prep(proj, None, mu_p, *prep_params, n_seq_tiles=nb, tiles_per_seq=t // tt_p, tt=tt_p, c=c, w=w, lp=lp, sh=1, hn=hn)
    rw, wkv_p = _wkv(streams, *wkv_params, None, nb=nb, t=t, rows=chunk, w=w, hn=hn, pp=pp, out_dtype=BF16)
    x1 = _mm_norm_res([cv, rw], [w_out_b[:c], w_out_b[c:]], g_mix_post, xp, tm=tm_e, tk=tk_e, name="mix_out_prompt")
    q = _norm_matmul(x1, g_xa_pre, w_q_b, tm=tm_a, tn=tn_a, out_dtype=BF16, name="q_prompt")
    o = _attn_prompt(q, mk.reshape(nb, n_mem, xd), mv.reshape(nb, n_mem, xd), nb, t, xa_heads, tq=min(tl["tq"], t))
    x2 = _mm_norm_res([o], [w_o_b], g_xa_post, x1, tm=tm_e, tk=tk_e, name="attn_out_prompt")
    yp, ffn_p = _ffn_prompt(x2, g_ffn_pre, w_up_b, cw3, cb2, w_down_b, g_ffn_post, nb=nb, t=t, tm=min(tl["tm_f"], t), tn=tn_f)
    shift_p = unpad_rcols(proj.reshape(nb, t, -1)[:, -1, 2 * c:])

    rows_s = t_s * db
    xs_bm = x_sample.reshape(db, t_s * d)
    proj_s = _norm_matmul(xs_bm, g_mix_pre, w_in_p, tm=rows_s, tn=tn_a, out_dtype=F32, name="proj_sample")
    cv_s, conv_s = _conv_sample(proj_s, state_conv[l].reshape(db, (kc - 1) * c), db, t_s, c, *conv_params)
    streams_s = _prep(proj_s, pad_rcols(state_shift[l]), mu_p, *prep_params, n_seq_tiles=1, tiles_per_seq=t_s, tt=db,
                      c=c, w=w, lp=lp, sh=db, hn=hn, out_batch_major=True)
    streams_s = [x.reshape(rows_s, w) for x in streams_s]
    rw_s, wkv_s = _wkv(streams_s, *wkv_params, state_wkv[l].reshape(db, w // LANES, LANES, hn), nb=db, t=t_s,
                       rows=tl["chunk"], w=w, hn=hn, pp=pp, out_dtype=F32)
    x1_s = _mm_norm_res([cv_s, rw_s.reshape(db, t_s * w)], [w_out_b[:c], w_out_b[c:]], g_mix_post, xs_bm, tm=db, tk=tk_e,
                        name="mix_out_sample", a_batch_major=(False, True), res_batch_major=True)
    q_s = _norm_matmul(x1_s, g_xa_pre, w_q_b, tm=db, tn=tn_a, out_dtype=F32, name="q_sample", out_batch_major=True)
    o_s = _attn_sample(q_s.reshape(rows_s, xd), cache_mem_k, cache_mem_v, db, t_s)
    x2_s = _mm_norm_res([o_s.reshape(db, t_s * xd)], [w_o_b], g_xa_post, x1_s, tm=db, tk=tk_e, name="attn_out_sample",
                        a_batch_major=(True,))
    ys2, ffn_s = _ffn_sample(x2_s, state_ffn[l].reshape(db, 4 * dff), g_ffn_pre, w_up_b, cw3, cb2, w_down_b, g_ffn_post,
                             db=db, t_s=t_s, tn=tn_f)
    shift_s = unpad_rcols(proj_s[(t_s - 1) * db:, 2 * c:])

    return (yp.reshape(nb, t, d), ys2.reshape(db, t_s, d),
            conv_p[None], conv_s.reshape(db, kc - 1, c)[None],
            shift_p[None], shift_s[None],
            wkv_p.reshape(nb, n_heads, hn, hn)[None], wkv_s.reshape(db, n_heads, hn, hn)[None],
            ffn_p[None], ffn_s[None],
            mk.reshape(nb, n_mem, xa_heads, xa_hd)[None], mv.reshape(nb, n_mem, xa_heads, xa_hd)[None])
```

```python
import functools
import math

import jax
import jax.numpy as jnp
from jax import lax
from jax.experimental import pallas as pl
from jax.experimental.pallas import tpu as pltpu

F32 = jnp.float32
BF16 = jnp.bfloat16
RMS_EPS = 1e-6
LN_EPS = 1e-5
GN_EPS = 64e-5
LANES = 128
ROW_CHUNK = 16
NORM_UNROLL = 4
FFN_SUB = 256
DECAY_SCALE = math.exp(-0.5)
_NT = (((1,), (1,)), ((), ()))


_TILES = dict(
    tm_a=1024, tn_a=512,
    tm_e=512, tk_e=2048,
    tt_conv=256, tt_prep=256,
    chunk=64, pp=8,
    tq=256,
    tm_f=512, tn_f=512,
)


def _cparams(n_grid, vmem_mib=48):
    return pltpu.CompilerParams(dimension_semantics=("arbitrary",) * n_grid,
                                vmem_limit_bytes=vmem_mib * 1024 * 1024)


def _sigmoid(x):
    return 1.0 / (1.0 + jnp.exp(-x))


def _rms(x, g):
    return x * lax.rsqrt(jnp.mean(x * x, axis=-1, keepdims=True) + RMS_EPS) * g


def _row_loop(n_rows, rc, fn, unroll=1):
    def body(i, carry):
        fn(pl.multiple_of(i * rc, rc))
        return carry
    lax.fori_loop(0, n_rows // rc, body, 0, unroll=unroll)


def _res_norm_rows(n_rows, g, load_y, load_res, store):
    n_grp = math.gcd(NORM_UNROLL, n_rows // ROW_CHUNK)

    def group(r0):
        rs = [pl.multiple_of(r0 + k * ROW_CHUNK, ROW_CHUNK) for k in range(n_grp)]
        ys = [load_y(r) for r in rs]
        res = [load_res(r) for r in rs]
        outs = [x + _rms(y, g) for x, y in zip(res, ys)]
        for r, o in zip(rs, outs):
            store(r, o)
    _row_loop(n_rows, ROW_CHUNK * n_grp, group)


def _mm(x, y, precision=None):
    return jnp.dot(x, y, precision=precision, preferred_element_type=F32)


def _norm_rows_to(h_scr, x_ref, g_ref):
    g = g_ref[...]
    rows = x_ref.shape[0]
    d = h_scr.shape[1]
    for t in range(x_ref.shape[1] // d):
        def chunk(r0, t=t):
            dst = pl.multiple_of(t * rows + r0, ROW_CHUNK)
            h_scr[pl.ds(dst, ROW_CHUNK), :] = _rms(x_ref[pl.ds(r0, ROW_CHUNK), t * d:(t + 1) * d], g).astype(BF16)
        _row_loop(rows, ROW_CHUNK, chunk, unroll=NORM_UNROLL)


def _norm_matmul_body(x_ref, g_ref, w_ref, o_ref, h_scr):
    @pl.when(pl.program_id(1) == 0)
    def _():
        _norm_rows_to(h_scr, x_ref, g_ref)

    o_ref[...] = _mm(h_scr[...], w_ref[...]).astype(o_ref.dtype)


def _norm_matmul(x, g, w, *, tm, tn, out_dtype, name, out_batch_major=False):
    d, n = w.shape
    n_t = x.shape[1] // d
    m = x.shape[0] * n_t
    assert n_t == 1 or m == tm
    tn = min(tn, n)
    nj = n // tn
    if out_batch_major:
        out_spec = pl.BlockSpec((tm, tn), lambda i, j: (0, i * nj + j))
        out_shape = jax.ShapeDtypeStruct((tm, (m // tm) * n), out_dtype)
    else:
        out_spec = pl.BlockSpec((tm, tn), lambda i, j: (i, j))
        out_shape = jax.ShapeDtypeStruct((m, n), out_dtype)
    return pl.pallas_call(
        _norm_matmul_body,
        grid=(m // tm, nj),
        in_specs=[pl.BlockSpec((tm // n_t, n_t * d), lambda i, j: (i, 0)),
                  pl.BlockSpec((1, d), lambda i, j: (0, 0)), pl.BlockSpec((d, tn), lambda i, j: (0, j))],
        out_specs=out_spec,
        out_shape=out_shape,
        scratch_shapes=[pltpu.VMEM((tm, d), BF16)],
        compiler_params=_cparams(2),
        name=name,
    )(x, g, w)


def _mm_norm_res_body(*refs, n_a):
    a_refs = refs[:n_a]
    w_refs = refs[n_a:2 * n_a]
    g_ref, r_ref, o_ref = refs[2 * n_a:2 * n_a + 3]
    k = pl.program_id(1)

    part = None
    for a, w in zip(a_refs, w_refs):
        d = _mm(a[...].astype(BF16), w[...])
        part = d if part is None else part + d

    @pl.when(k == 0)
    def _():
        o_ref[...] = part

    @pl.when(k > 0)
    def _():
        o_ref[...] += part

    @pl.when(k == pl.num_programs(1) - 1)
    def _():
        def store(r, v):
            o_ref[pl.ds(r, ROW_CHUNK), :] = v
        _res_norm_rows(o_ref.shape[0], g_ref[...], lambda r: o_ref[pl.ds(r, ROW_CHUNK), :],
                       lambda r: r_ref[pl.ds(r, ROW_CHUNK), :], store)


def _mm_norm_res(a_list, w_list, g, res, *, tm, tk, name, a_batch_major=(), res_batch_major=False):
    kdim, n = w_list[0].shape
    n_a = len(a_list)
    m = res.size // n
    tk = min(tk, kdim)
    nk = kdim // tk
    a_specs = []
    for idx in range(n_a):
        if idx < len(a_batch_major) and a_batch_major[idx]:
            a_specs.append(pl.BlockSpec((tm, tk), lambda i, k: (0, i * nk + k)))
        else:
            a_specs.append(pl.BlockSpec((tm, tk), lambda i, k: (i, k)))
    return pl.pallas_call(
        functools.partial(_mm_norm_res_body, n_a=n_a),
        grid=(m // tm, nk),
        in_specs=[*a_specs,
                  *[pl.BlockSpec((tk, n), lambda i, k: (k, 0)) for _ in w_list],
                  pl.BlockSpec((1, n), lambda i, k: (0, 0)),
                  pl.BlockSpec((tm, n), (lambda i, k: (0, i)) if res_batch_major else (lambda i, k: (i, 0)))],
        out_specs=pl.BlockSpec((tm, n), lambda i, k: (i, 0)),
        out_shape=jax.ShapeDtypeStruct((m, n), F32),
        compiler_params=_cparams(2),
        name=name,
    )(*a_list, *w_list, g, res)


def _ln_swish(y, lg, lb):
    mu = jnp.mean(y, axis=-1, keepdims=True)
    d = y - mu
    var = jnp.mean(d * d, axis=-1, keepdims=True)
    yn = d * lax.rsqrt(var + LN_EPS) * lg + lb
    return yn * _sigmoid(yn)


def _conv_prompt_body(a_ref, b_ref, w_ref, cb_ref, lg_ref, lb_ref, cv_ref, nc_ref, full, cvt, *, kw, lane_blk):
    tt, c = a_ref.shape
    past = kw - 1
    hp = -(-past // 8) * 8
    lead = hp - past
    ti = pl.program_id(1)

    @pl.when(ti == 0)
    def _():
        full[0:hp, :] = jnp.zeros((hp, c), F32)

    @pl.when(ti > 0)
    def _():
        full[0:hp, :] = full[tt:tt + hp, :]

    def glu(r0):
        dst = pl.multiple_of(hp + r0, ROW_CHUNK)
        full[pl.ds(dst, ROW_CHUNK), :] = a_ref[pl.ds(r0, ROW_CHUNK), :] * _sigmoid(b_ref[pl.ds(r0, ROW_CHUNK), :])
    _row_loop(tt, ROW_CHUNK, glu)

    rt = 32

    def taps(r0):
        for lb in range(c // lane_blk):
            ls = slice(lb * lane_blk, (lb + 1) * lane_blk)
            win = full[pl.ds(r0, rt + hp), ls]
            acc = jnp.broadcast_to(cb_ref[:, ls], (rt, lane_blk))
            for s in range(8):
                offs = [o for o in range(lead, lead + kw) if o % 8 == s]
                if not offs:
                    continue
                span = offs[-1] - s + rt
                shifted = win if s == 0 else win[s:s + span, :]
                for o in offs:
                    acc = acc + shifted[o - s:o - s + rt, :] * w_ref[o - lead:o - lead + 1, ls]
            cvt[pl.ds(r0, rt), ls] = acc
    _row_loop(tt, rt, taps)

    def ln(r0):
        cv_ref[pl.ds(r0, ROW_CHUNK), :] = _ln_swish(cvt[pl.ds(r0, ROW_CHUNK), :], lg_ref[...], lb_ref[...]).astype(cv_ref.dtype)
    _row_loop(tt, ROW_CHUNK, ln, unroll=2)

    @pl.when(ti == pl.num_programs(1) - 1)
    def _():
        nc_ref[...] = full[tt + lead:tt + hp, :]


def _conv_prompt(proj, nb, t, c, conv_w, conv_b, ln_g, ln_b, tt):
    kw = conv_w.shape[0]
    nt = t // tt
    hp = -(-(kw - 1) // 8) * 8
    vec = pl.BlockSpec((1, c), lambda b, i: (0, 0))
    return pl.pallas_call(
        functools.partial(_conv_prompt_body, kw=kw, lane_blk=LANES),
        grid=(nb, nt),
        in_specs=[pl.BlockSpec((tt, c), lambda b, i: (b * nt + i, 0)),
                  pl.BlockSpec((tt, c), lambda b, i: (b * nt + i, 1)),
                  pl.BlockSpec((kw, c), lambda b, i: (0, 0)), vec, vec, vec],
        out_specs=[pl.BlockSpec((tt, c), lambda b, i: (b * nt + i, 0)),
                   pl.BlockSpec((None, kw - 1, c), lambda b, i: (b, 0, 0))],
        out_shape=[jax.ShapeDtypeStruct((nb * t, c), BF16), jax.ShapeDtypeStruct((nb, kw - 1, c), F32)],
        scratch_shapes=[pltpu.VMEM((tt + hp, c), F32), pltpu.VMEM((tt, c), F32)],
        compiler_params=_cparams(2),
        name="conv_prompt",
    )(proj, proj, conv_w, conv_b, ln_g, ln_b)


def _conv_sample_body(a_ref, b_ref, st_ref, w_ref, cb_ref, lg_ref, lb_ref, cv_ref, nc_ref, glu, cvt, *, t_s, kw):
    db = st_ref.shape[0]
    c = w_ref.shape[1]
    past = kw - 1
    rows = t_s * db

    def make_glu(r0):
        glu[pl.ds(r0, ROW_CHUNK), :] = a_ref[pl.ds(r0, ROW_CHUNK), :] * _sigmoid(b_ref[pl.ds(r0, ROW_CHUNK), :])
    _row_loop(rows, ROW_CHUNK, make_glu)

    def taps(r0):
        for lb in range(c // LANES):
            ls = slice(lb * LANES, (lb + 1) * LANES)
            srcs = []
            for f in range(past + t_s):
                if f < past:
                    srcs.append(st_ref[pl.ds(r0, 8), f * c + ls.start:f * c + ls.stop])
                else:
                    srcs.append(glu[pl.ds(pl.multiple_of((f - past) * db + r0, 8), 8), ls])
            for t in range(t_s):
                acc = jnp.broadcast_to(cb_ref[:, ls], (8, LANES))
                for j in range(kw):
                    acc = acc + srcs[t + j] * w_ref[j:j + 1, ls]
                cvt[pl.ds(pl.multiple_of(t * db + r0, 8), 8), ls] = acc
    _row_loop(db, 8, taps)

    def ln(r0):
        cv_ref[pl.ds(r0, ROW_CHUNK), :] = _ln_swish(cvt[pl.ds(r0, ROW_CHUNK), :], lg_ref[...], lb_ref[...]).astype(cv_ref.dtype)
    _row_loop(rows, ROW_CHUNK, ln)

    for r in range(past):
        f = t_s + r
        if f < past:
            nc_ref[:, r * c:(r + 1) * c] = st_ref[:, f * c:(f + 1) * c]
        else:
            nc_ref[:, r * c:(r + 1) * c] = glu[(f - past) * db:(f - past + 1) * db, :]


def _conv_sample(proj, state2d, db, t_s, c, conv_w, conv_b, ln_g, ln_b):
    kw = conv_w.shape[0]
    rows = t_s * db
    vec = pl.BlockSpec((1, c), lambda i: (0, 0))
    return pl.pallas_call(
        functools.partial(_conv_sample_body, t_s=t_s, kw=kw),
        grid=(1,),
        in_specs=[pl.BlockSpec((rows, c), lambda i: (0, 0)),
                  pl.BlockSpec((rows, c), lambda i: (0, 1)),
                  pl.BlockSpec((db, (kw - 1) * c), lambda i: (0, 0)),
                  pl.BlockSpec((kw, c), lambda i: (0, 0)), vec, vec, vec],
        out_specs=[pl.BlockSpec((rows, c), lambda i: (0, 0)),
                   pl.BlockSpec((db, (kw - 1) * c), lambda i: (0, 0))],
        out_shape=[jax.ShapeDtypeStruct((rows, c), BF16), jax.ShapeDtypeStruct((db, (kw - 1) * c), F32)],
        scratch_shapes=[pltpu.VMEM((rows, c), F32), pltpu.VMEM((rows, c), F32)],
        compiler_params=_cparams(1, 56),
        name="conv_sample",
    )(proj, proj, state2d, conv_w, conv_b, ln_g, ln_b)


def _head_ones(hn):
    ri = lax.broadcasted_iota(jnp.int32, (LANES, LANES), 0)
    ci = lax.broadcasted_iota(jnp.int32, (LANES, LANES), 1)
    return (ri // hn) == (ci // hn)


def _prep_body(*refs, sh, has_state, hn, dlp, alp):
    pr = refs[0:4]
    i0 = 4
    st = None
    if has_state:
        st = refs[i0:i0 + 4]
        i0 += 4
    mu = refs[i0:i0 + 4]
    w0_ref, a0_ref, kk_ref, ka_ref, wl_ref, al_ref, gl_ref = refs[i0 + 4:i0 + 11]
    r_ref, lw_ref, k2_ref, v_ref, a_ref, b_ref, g_ref = refs[i0 + 11:i0 + 18]
    bufs = refs[i0 + 18:i0 + 22]
    lact = refs[i0 + 22]
    tt, w = r_ref.shape
    off = -(-sh // 8) * 8
    ti = pl.program_id(1)

    @pl.when(ti == 0)
    def _():
        for n, bf in enumerate(bufs):
            bf[0:off, :] = st[n][...] if has_state else jnp.zeros((off, bf.shape[1]), F32)

    @pl.when(ti > 0)
    def _():
        for bf in bufs:
            bf[0:off, :] = bf[tt:tt + off, :]

    def copy(r0):
        dst = pl.multiple_of(off + r0, 8)
        for bf, p in zip(bufs, pr):
            bf[pl.ds(dst, 8), :] = p[pl.ds(r0, 8), :]
    _row_loop(tt, 8, copy)

    def shifted(bf, m, r0):
        cur = bf[pl.ds(pl.multiple_of(off + r0, 8), 8), :]
        if sh % 8 == 0:
            prev = bf[pl.ds(pl.multiple_of(off - sh + r0, 8), 8), :]
        else:
            win = bf[pl.ds(r0, 16), :]
            prev = win[8 - sh:16 - sh, :]
        return cur + (prev - cur) * m[...]

    def stage1(r0):
        r_ref[pl.ds(r0, 8), :] = shifted(bufs[0], mu[0], r0)
        k = shifted(bufs[1], mu[1], r0)
        k2_ref[pl.ds(r0, 8), :] = k
        kk = k * kk_ref[...]
        b_ref[pl.ds(r0, 8), :] = kk * kk
        v_ref[pl.ds(r0, 8), :] = shifted(bufs[2], mu[2], r0)
    _row_loop(tt, 8, stage1)

    def stage1l(r0):
        xl = None
        for h in range(2):
            part = shifted(bufs[3], mu[3], r0 + 8 * h)
            xl = part if xl is None else jnp.concatenate([xl, part], axis=0)
        lane = lax.broadcasted_iota(jnp.int32, xl.shape, 1)
        act = jnp.where(lane < dlp, jnp.tanh(xl), jnp.where(lane < dlp + alp, xl, _sigmoid(xl)))
        lact[pl.ds(r0, ROW_CHUNK), :] = act.astype(BF16)
    _row_loop(tt, ROW_CHUNK, stage1l)

    lw_ref[...] = _mm(lact[:, 0:dlp], wl_ref[...])
    a_ref[...] = _mm(lact[:, dlp:dlp + alp], al_ref[...])
    g_ref[...] = _mm(lact[:, dlp + alp:], gl_ref[...])
    ones = jnp.where(_head_ones(hn), 1.0, 0.0).astype(BF16)
    for q in range(w // LANES):
        ls = slice(q * LANES, (q + 1) * LANES)
        b_ref[:, ls] = _dot_exact_rhs(_split3(b_ref[:, ls]), ones)

    def stage3(r0):
        rs = pl.ds(r0, 8)
        k = k2_ref[rs, :]
        lw_ref[rs, :] = -DECAY_SCALE * _sigmoid(w0_ref[...] + lw_ref[rs, :])
        asig = _sigmoid(a0_ref[...] + a_ref[rs, :])
        kk = k * kk_ref[...]
        kkn = kk / jnp.maximum(jnp.sqrt(b_ref[rs, :]), 1e-12)
        k2_ref[rs, :] = k * (1.0 + (asig - 1.0) * ka_ref[...])
        a_ref[rs, :] = -kkn
        b_ref[rs, :] = kkn * asig
    _row_loop(tt, 8, stage3)


def _prep(proj, st, mu_p, w0, a0, k_k, k_a, wl, al, gl, *, n_seq_tiles, tiles_per_seq, tt, c, w, lp, sh, hn,
          out_batch_major=False):
    dlp, alp = wl.shape[0], al.shape[0]
    has_state = st is not None
    rows = n_seq_tiles * tiles_per_seq * tt
    cb = 2 * c // w
    lb = (2 * c + 3 * w) // lp
    off = -(-sh // 8) * 8
    rmap = lambda o: (lambda s, i: (s * tiles_per_seq + i, o))
    in_specs = [pl.BlockSpec((tt, w), rmap(cb)), pl.BlockSpec((tt, w), rmap(cb + 1)),
                pl.BlockSpec((tt, w), rmap(cb + 2)), pl.BlockSpec((tt, lp), rmap(lb))]
    args = [proj, proj, proj, proj]
    if has_state:
        in_specs += [pl.BlockSpec((sh, w), lambda s, i: (0, 0)), pl.BlockSpec((sh, w), lambda s, i: (0, 1)),
                     pl.BlockSpec((sh, w), lambda s, i: (0, 2)), pl.BlockSpec((sh, lp), lambda s, i: (0, 3 * w // lp))]
        args += [st, st, st, st]
    in_specs += [pl.BlockSpec((1, w), lambda s, i: (0, 0)), pl.BlockSpec((1, w), lambda s, i: (0, 1)),
                 pl.BlockSpec((1, w), lambda s, i: (0, 2)), pl.BlockSpec((1, lp), lambda s, i: (0, 3 * w // lp))]
    args += [mu_p, mu_p, mu_p, mu_p]
    vec = pl.BlockSpec((1, w), lambda s, i: (0, 0))
    in_specs += [vec, vec, vec, vec,
                 pl.BlockSpec(wl.shape, lambda s, i: (0, 0)), pl.BlockSpec(al.shape, lambda s, i: (0, 0)),
                 pl.BlockSpec(gl.shape, lambda s, i: (0, 0))]
    args += [w0, a0, k_k, k_a, wl, al, gl]
    if out_batch_major:
        assert n_seq_tiles == 1
        ospec = pl.BlockSpec((tt, w), lambda s, i: (0, i))
        oshape = jax.ShapeDtypeStruct((tt, tiles_per_seq * w), F32)
    else:
        ospec = pl.BlockSpec((tt, w), lambda s, i: (s * tiles_per_seq + i, 0))
        oshape = jax.ShapeDtypeStruct((rows, w), F32)
    return pl.pallas_call(
        functools.partial(_prep_body, sh=sh, has_state=has_state, hn=hn, dlp=dlp, alp=alp),
        grid=(n_seq_tiles, tiles_per_seq),
        in_specs=in_specs,
        out_specs=[ospec] * 7,
        out_shape=[oshape] * 7,
        scratch_shapes=[pltpu.VMEM((off + tt, w), F32)] * 3 + [pltpu.VMEM((off + tt, lp), F32), pltpu.VMEM((tt, lp), BF16)],
        compiler_params=_cparams(2, 56),
        name="rwkv_prep_state" if has_state else "rwkv_prep",
    )(*args)


def _split2(x):
    hi = x.astype(BF16)
    lo = (x - hi.astype(F32)).astype(BF16)
    return hi, lo


def _split3(x):
    hi = x.astype(BF16)
    r1 = x - hi.astype(F32)
    mid = r1.astype(BF16)
    lo = (r1 - mid.astype(F32)).astype(BF16)
    return hi, mid, lo


def _dot3(xs, ys, dims=(((1,), (0,)), ((), ()))):
    def d(a, b):
        return lax.dot_general(a, b, dims, preferred_element_type=F32)
    return d(xs[0], ys[0]) + (d(xs[0], ys[1]) + d(xs[1], ys[0]))


def _dot_exact_rhs(parts, y_bf16):
    out = None
    for p in parts[::-1]:
        d = _mm(p, y_bf16)
        out = d if out is None else out + d
    return out


def _wkv_body(*refs, sub, pp, has_state, hn):
    r_ref, lw_ref, k_ref, v_ref, a_ref, b_ref, g_ref, rk_ref, lg_ref, lb_ref = refs[:10]
    i0 = 10
    s0_ref = None
    if has_state:
        s0_ref = refs[i0]
        i0 += 1
    o_ref, so_ref, sbd = refs[i0:i0 + 3]
    ell = r_ref.shape[0]
    l2 = 2 * ell
    nseq = ell // sub
    n_iter = max(1, int(math.log2(sub)))
    c = pl.program_id(2)
    bd = _head_ones(hn)
    lane = lax.broadcasted_iota(jnp.int32, (1, LANES), 1)
    lo = lane < hn
    row_lo = lax.broadcasted_iota(jnp.int32, (LANES, LANES), 0) < hn

    @pl.when(c == 0)
    def _():
        if has_state:
            for s in range(nseq):
                for q in range(pp):
                    sbd[s * pp + q] = jnp.zeros((LANES, LANES), F32)
                    sbd[s * pp + q, :, 0:hn] = s0_ref[s, q]
                    x = sbd[s * pp + q]
                    sbd[s * pp + q] = jnp.where(bd, x + pltpu.roll(x, hn, 1), 0.0)
        else:
            sbd[...] = jnp.zeros(sbd.shape, F32)

    lw = lw_ref[...]
    ri = lax.broadcasted_iota(jnp.int32, (ell, ell), 0)
    ci = lax.broadcasted_iota(jnp.int32, (ell, ell), 1)
    tri = jnp.where((ri >= ci) & ((ri // sub) == (ci // sub)), 1.0, 0.0).astype(BF16)
    cs = None
    for part in _split3(lw)[::-1]:
        d = _mm(tri, part)
        cs = d if cs is None else cs + d
    w_in = jnp.exp(cs)
    w_inv = jnp.exp(-cs)
    a_t = a_ref[...] * jnp.exp(cs - lw)
    r_t = r_ref[...] * w_in
    b_t = b_ref[...] * w_inv
    k_t = k_ref[...] * w_inv
    rr = lax.broadcasted_iota(jnp.int32, (l2, l2), 0) % ell
    cc = lax.broadcasted_iota(jnp.int32, (l2, l2), 1) % ell
    same_seq = (rr // sub) == (cc // sub)
    strict = (cc < rr) & same_seq
    incl = (cc <= rr) & same_seq
    ones = jnp.where(bd, 1.0, 0.0).astype(BF16)
    avg = jnp.where(bd, 1.0 / hn, 0.0).astype(BF16)

    def seq_rows(x, s):
        return jnp.concatenate([x[s * sub:(s + 1) * sub], x[ell + s * sub:ell + (s + 1) * sub]], axis=0)

    def head_stack(pieces):
        return jnp.concatenate([p[0:sub] for p in pieces] + [p[sub:2 * sub] for p in pieces], axis=0)

    pairs = range(pp)
    lss = [slice(q * LANES, (q + 1) * LANES) for q in pairs]

    def blk(x, q):
        xs = x[:, lss[q]]
        return jnp.concatenate([jnp.where(lo, xs, 0.0), jnp.where(lo, 0.0, xs)], axis=0)

    ab = [blk(a_t, q) for q in pairs]
    rb = [blk(r_t, q) for q in pairs]
    bb = [blk(b_t, q) for q in pairs]
    kb = [blk(k_t, q) for q in pairs]
    vb = [blk(v_ref[...], q) for q in pairs]
    a_s = [_split2(ab[q]) for q in pairs]
    r_hi = [rb[q].astype(BF16) for q in pairs]
    bk_s = [_split2(jnp.concatenate([bb[q], kb[q]], axis=0)) for q in pairs]
    v_s = [_split2(vb[q]) for q in pairs]
    g_a = [_dot3(a_s[q], bk_s[q], _NT) for q in pairs]
    g_r = [lax.dot_general(r_hi[q], bk_s[q][0], _NT, preferred_element_type=F32) for q in pairs]
    m_s = [_split2(jnp.where(strict, g_a[q][:, 0:l2], 0.0)) for q in pairs]
    m_ak = [_split2(jnp.where(strict, g_a[q][:, l2:], 0.0)) for q in pairs]
    m_r = [jnp.concatenate([jnp.where(incl, g_r[q][:, 0:l2], 0.0), jnp.where(incl, g_r[q][:, l2:], 0.0)],
                           axis=1).astype(BF16) for q in pairs]

    ps_a, ps_r = [], []
    for q in pairs:
        if nseq == 1:
            s_s = _split2(sbd[q])
            ps_a.append(_dot3(a_s[q], s_s, _NT))
            ps_r.append(lax.dot_general(r_hi[q], s_s[0], _NT, preferred_element_type=F32))
        else:
            pa, pr = [], []
            for s in range(nseq):
                sel = jnp.concatenate([seq_rows(ab[q], s), seq_rows(rb[q], s)], axis=0)
                ps = _dot3(_split2(sel), _split2(sbd[s * pp + q]), _NT)
                pa.append(ps[0:2 * sub])
                pr.append(ps[2 * sub:])
            ps_a.append(head_stack(pa))
            ps_r.append(head_stack(pr))

    u = [ps_a[q] + _dot3(m_ak[q], v_s[q]) for q in pairs]
    for it in range(n_iter):
        u = [u[q] + _dot_exact_rhs(m_s[q], u[q].astype(BF16)) for q in pairs]
        if it < n_iter - 1:
            m_s = [_split2(_dot3(m_s[q], m_s[q])) for q in pairs]
    uv = [jnp.concatenate([u[q], vb[q]], axis=0) for q in pairs]
    yb = [ps_r[q] + _mm(m_r[q], uv[q].astype(BF16)) for q in pairs]
    y = [yb[q][0:ell, :] + yb[q][ell:l2, :] for q in pairs]

    for q in pairs:
        for s in range(nseq):
            if nseq == 1:
                uv_t, bk_sel = uv[q].T, bk_s[q]
            else:
                uv_t = jnp.concatenate([seq_rows(u[q], s), seq_rows(vb[q], s)], axis=0).T
                bk_sel = _split2(jnp.concatenate([seq_rows(bb[q], s), seq_rows(kb[q], s)], axis=0))
            upd = _dot3(_split2(uv_t), bk_sel)
            last = (s + 1) * sub - 1
            sbd[s * pp + q] = (sbd[s * pp + q] + upd) * w_in[last:last + 1, lss[q]]

    mu = [_dot_exact_rhs(_split2(y[q]), avg) for q in pairs]
    d = [y[q] - mu[q] for q in pairs]
    var = [_dot_exact_rhs(_split2(d[q] * d[q]), avg) for q in pairs]
    bonus = [_dot_exact_rhs(_split2(r_ref[:, lss[q]] * k_ref[:, lss[q]] * rk_ref[:, lss[q]]), ones) for q in pairs]
    for q in pairs:
        ls = lss[q]
        yn = d[q] * lax.rsqrt(var[q] + GN_EPS) * lg_ref[:, ls] + lb_ref[:, ls]
        o_ref[:, ls] = ((yn + bonus[q] * v_ref[:, ls]) * g_ref[:, ls]).astype(o_ref.dtype)

    @pl.when(c == pl.num_programs(2) - 1)
    def _():
        for s in range(nseq):
            for q in range(pp):
                x = sbd[s * pp + q]
                so_ref[s, q] = jnp.where(row_lo, x, pltpu.roll(x, hn, 1))[:, 0:hn]


def _wkv(streams, r_k, ln_g, ln_b, s0, *, nb, t, rows, w, hn, pp, out_dtype):
    pw = pp * LANES
    ng = w // pw
    has_state = s0 is not None
    sub = min(t, rows)
    nseq = rows // sub
    n_chunks = t // sub
    assert n_chunks == 1 or nseq == 1
    sspec = pl.BlockSpec((rows, pw), lambda b, g, c: (b * n_chunks + c, g))
    vec = pl.BlockSpec((1, pw), lambda b, g, c: (0, g))
    stspec = pl.BlockSpec((nseq, pp, LANES, hn), lambda b, g, c: (b, g, 0, 0))
    in_specs = [sspec] * 7 + [vec] * 3 + ([stspec] if has_state else [])
    args = list(streams) + [r_k, ln_g, ln_b] + ([s0] if has_state else [])
    out, s_new = pl.pallas_call(
        functools.partial(_wkv_body, sub=sub, pp=pp, has_state=has_state, hn=hn),
        grid=(nb // nseq, ng, n_chunks),
        in_specs=in_specs,
        out_specs=[sspec, stspec],
        out_shape=[jax.ShapeDtypeStruct((nb * t, w), out_dtype), jax.ShapeDtypeStruct((nb, w // LANES, LANES, hn), F32)],
        scratch_shapes=[pltpu.VMEM((nseq * pp, LANES, LANES), F32)],
        compiler_params=_cparams(3),
        name="wkv_state" if has_state else "wkv",
    )(*args)
    return out, s_new


def _attn_body(q_ref, k_ref, v_ref, o_ref, kb_ref, vb_ref, *, nh, scale):
    @pl.when(pl.program_id(1) == 0)
    def _():
        kb_ref[...] = k_ref[...].astype(BF16)
        vb_ref[...] = v_ref[...].astype(BF16)
    hd = q_ref.shape[1] // nh
    for h in range(nh):
        hs = slice(h * hd, (h + 1) * hd)
        kh, vh = kb_ref[:, hs], vb_ref[:, hs]
        s = lax.dot_general(q_ref[:, hs].astype(BF16), kh, _NT, preferred_element_type=F32) * scale
        p = jnp.exp(s - jnp.max(s, axis=-1, keepdims=True))
        p = p / jnp.sum(p, axis=-1, keepdims=True)
        o_ref[:, hs] = _mm(p.astype(BF16), vh).astype(o_ref.dtype)


def _attn_prompt(q, k, v, nb, t, nh, tq):
    nm, xd = k.shape[1], k.shape[2]
    nt = t // tq
    kv = pl.BlockSpec((None, nm, xd), lambda b, i: (b, 0, 0))
    return pl.pallas_call(
        functools.partial(_attn_body, nh=nh, scale=(xd // nh) ** -0.5),
        grid=(nb, nt),
        in_specs=[pl.BlockSpec((tq, xd), lambda b, i: (b * nt + i, 0)), kv, kv],
        out_specs=pl.BlockSpec((tq, xd), lambda b, i: (b * nt + i, 0)),
        out_shape=jax.ShapeDtypeStruct((nb * t, xd), BF16),
        scratch_shapes=[pltpu.VMEM((nm, xd), BF16)] * 2,
        compiler_params=_cparams(2),
        name="attn_prompt",
    )(q, k, v)


def _attn_sample_body(q_ref, k_ref, v_ref, o_ref, *, scale):
    nm, nh, hd = k_ref.shape
    t_s = q_ref.shape[0]
    k2 = k_ref[...].reshape(nm * nh, hd).astype(BF16)
    v2 = v_ref[...].reshape(nm * nh, hd).astype(BF16)
    q4 = jnp.concatenate([q_ref[:, h * hd:(h + 1) * hd] for h in range(nh)], axis=0).astype(BF16)
    s = lax.dot_general(q4, k2, _NT, preferred_element_type=F32) * scale
    row_head = lax.broadcasted_iota(jnp.int32, s.shape, 0) // t_s
    col_head = lax.broadcasted_iota(jnp.int32, s.shape, 1) % nh
    s = jnp.where(row_head == col_head, s, -1e30)
    p = jnp.exp(s - jnp.max(s, axis=-1, keepdims=True))
    p = p / jnp.sum(p, axis=-1, keepdims=True)
    o4 = _mm(p.astype(BF16), v2)
    for h in range(nh):
        o_ref[:, h * hd:(h + 1) * hd] = o4[h * t_s:(h + 1) * t_s, :]


def _attn_sample(q, k, v, db, t_s):
    _, _, nm, nh, hd = k.shape
    xd = nh * hd
    qs = pl.BlockSpec((t_s, xd), lambda b: (b, 0))
    kv = pl.BlockSpec((None, None, nm, nh, hd), lambda b: (0, b, 0, 0, 0))
    return pl.pallas_call(
        functools.partial(_attn_sample_body, scale=hd ** -0.5),
        grid=(db,),
        in_specs=[qs, kv, kv],
        out_specs=qs,
        out_shape=jax.ShapeDtypeStruct((db * t_s, xd), F32),
        compiler_params=_cparams(1),
        name="attn_sample",
    )(q, k, v)


def _ffn_body(*refs, sh, t_s, has_state, tiles_per_seq):
    x_ref, gpre_ref, wua_ref, wub_ref, cw_ref, cb_ref, wd_ref, gpost_ref = refs[:8]
    i0 = 8
    st = None
    if has_state:
        st = refs[i0:i0 + 4]
        i0 += 4
    o_ref = refs[i0]
    i0 += 1
    n_nf = 4 if has_state else 1
    nf_refs = refs[i0:i0 + n_nf]
    i0 += n_nf
    hn_scr = refs[i0]
    nsb = (len(refs) - i0 - 1 - (0 if has_state else 1)) // 3
    fas = refs[i0 + 1:i0 + 1 + nsb]
    fbs = refs[i0 + 1 + nsb:i0 + 1 + 2 * nsb]
    acts = refs[i0 + 1 + 2 * nsb:i0 + 1 + 3 * nsb]
    carry = None if has_state else refs[i0 + 1 + 3 * nsb]
    tm, d = x_ref.shape
    sb = fas[0].shape[1]
    tn = nsb * sb
    off = max(8, 2 * sh)
    n_t = o_ref.shape[1] // d
    rows = tm // n_t
    i = pl.program_id(0)
    j = pl.program_id(1)

    @pl.when(j == 0)
    def _():
        _norm_rows_to(hn_scr, x_ref, gpre_ref)
        o_ref[...] = jnp.zeros(o_ref.shape, F32)

    if has_state:
        for s in range(nsb):
            cs = slice(s * sb, (s + 1) * sb)
            fas[s][0:sh, :] = st[0][:, cs]
            fbs[s][0:sh, :] = st[1][:, cs]
            fas[s][sh:2 * sh, :] = st[2][:, cs]
            fbs[s][sh:2 * sh, :] = st[3][:, cs]
    else:
        first = (i % tiles_per_seq) == 0

        @pl.when(first)
        def _():
            for s in range(nsb):
                fas[s][0:off, :] = jnp.zeros((off, sb), F32)
                fbs[s][0:off, :] = jnp.zeros((off, sb), F32)

        @pl.when(jnp.logical_not(first))
        def _():
            for s in range(nsb):
                fas[s][0:off, :] = carry[j, :, s * sb:(s + 1) * sb]
                fbs[s][0:off, :] = carry[j, :, tn + s * sb:tn + (s + 1) * sb]

    def up(s):
        cs = slice(s * sb, (s + 1) * sb)
        fas[s][off:off + tm, :] = _mm(hn_scr[...], wua_ref[:, cs])
        fbs[s][off:off + tm, :] = _mm(hn_scr[...], wub_ref[:, cs])

    def conv_act(s):
        cs = slice(s * sb, (s + 1) * sb)

        def conv(f, h, r0):
            if sh % 8 == 0:
                s2 = f[off - 2 * sh + r0:off - 2 * sh + r0 + ROW_CHUNK, :]
                s1 = f[off - sh + r0:off - sh + r0 + ROW_CHUNK, :]
                s0 = f[off + r0:off + r0 + ROW_CHUNK, :]
            else:
                win = f[r0:r0 + ROW_CHUNK + 8, :]
                s2 = win[8 - 2 * sh:8 - 2 * sh + ROW_CHUNK, :]
                s1 = win[8 - sh:8 - sh + ROW_CHUNK, :]
                s0 = win[8:8 + ROW_CHUNK, :]
            return (cb_ref[h:h + 1, cs] + s2 * cw_ref[0, h:h + 1, cs] + s1 * cw_ref[1, h:h + 1, cs]
                    + s0 * cw_ref[2, h:h + 1, cs])
        for r0 in range(0, tm, ROW_CHUNK):
            ua = conv(fas[s], 0, r0)
            ub = conv(fbs[s], 1, r0)
            acts[s][r0:r0 + ROW_CHUNK, :] = (ua * _sigmoid(ua) * ub).astype(BF16)

    def down(s):
        part = _mm(acts[s][...], wd_ref[s * sb:(s + 1) * sb, :])
        for t in range(n_t):
            o_ref[:, t * d:(t + 1) * d] += part[t * rows:(t + 1) * rows, :]

    up(0)
    for s in range(1, nsb):
        up(s)
        conv_act(s - 1)
        down(s - 1)
    conv_act(nsb - 1)
    down(nsb - 1)

    if not has_state:
        for s in range(nsb):
            carry[j, :, s * sb:(s + 1) * sb] = fas[s][tm:tm + off, :]
            carry[j, :, tn + s * sb:tn + (s + 1) * sb] = fbs[s][tm:tm + off, :]

    for s in range(nsb):
        cs = slice(s * sb, (s + 1) * sb)
        if has_state:
            for r in range(2):
                src = off + (t_s - 2 + r) * sh
                nf_refs[2 * r][:, cs] = fas[s][src:src + sh, :]
                nf_refs[2 * r + 1][:, cs] = fbs[s][src:src + sh, :]
        else:
            nf = nf_refs[0]
            for r in range(2):
                src = off + tm - 2 + r
                nf[r, 0:1, cs] = fas[s][src:src + 1, :]
                nf[r, 1:2, cs] = fbs[s][src:src + 1, :]

    @pl.when(j == pl.num_programs(1) - 1)
    def _():
        for t in range(n_t):
            ts = slice(t * d, (t + 1) * d)

            def store(r, v, ts=ts):
                o_ref[pl.ds(r, ROW_CHUNK), ts] = v
            _res_norm_rows(rows, gpost_ref[...], lambda r, ts=ts: o_ref[pl.ds(r, ROW_CHUNK), ts],
                           lambda r, t=t: x_ref[pl.ds(pl.multiple_of(t * rows + r, ROW_CHUNK), ROW_CHUNK), :], store)


def _ffn_prompt(x, g_pre, w_up, cw3, cb2, w_down, g_post, *, nb, t, tm, tn):
    d = x.shape[1]
    dff = w_down.shape[0]
    nj = dff // tn
    sb = min(FFN_SUB, tn)
    nsb = tn // sb
    tps = t // tm
    vec = pl.BlockSpec((1, d), lambda i, j: (0, 0))
    out, nf = pl.pallas_call(
        functools.partial(_ffn_body, sh=1, t_s=None, has_state=False, tiles_per_seq=tps),
        grid=(nb * tps, nj),
        in_specs=[pl.BlockSpec((tm, d), lambda i, j: (i, 0)), vec,
                  pl.BlockSpec((d, tn), lambda i, j: (0, j)), pl.BlockSpec((d, tn), lambda i, j: (0, j + nj)),
                  pl.BlockSpec((3, 2, tn), lambda i, j: (0, 0, j)), pl.BlockSpec((2, tn), lambda i, j: (0, j)),
                  pl.BlockSpec((tn, d), lambda i, j: (j, 0)), vec],
        out_specs=[pl.BlockSpec((tm, d), lambda i, j: (i, 0)),
                   pl.BlockSpec((None, 2, 2, tn), lambda i, j: (i, 0, 0, j))],
        out_shape=[jax.ShapeDtypeStruct((nb * t, d), F32), jax.ShapeDtypeStruct((nb * tps, 2, 2, dff), F32)],
        scratch_shapes=[pltpu.VMEM((tm, d), BF16),
                        *[pltpu.VMEM((8 + tm, sb), F32)] * (2 * nsb), *[pltpu.VMEM((tm, sb), BF16)] * nsb,
                        pltpu.VMEM((nj, 8, 2 * tn), F32)],
        compiler_params=_cparams(2, 56),
        name="ffn_prompt",
    )(x, g_pre, w_up, w_up, cw3, cb2, w_down, g_post)
    return out, nf.reshape(nb, tps, 2, 2 * dff)[:, -1]


def _ffn_sample(x_tm, st2d, g_pre, w_up, cw3, cb2, w_down, g_post, *, db, t_s, tn):
    d = x_tm.shape[1]
    dff = w_down.shape[0]
    nj = dff // tn
    sb = min(FFN_SUB, tn)
    nsb = tn // sb
    tm = t_s * db
    vec = pl.BlockSpec((1, d), lambda i, j: (0, 0))
    st_specs = [pl.BlockSpec((db, tn), (lambda i, j, o=o: (0, o * nj + j))) for o in range(4)]
    nf_spec = pl.BlockSpec((db, tn), lambda i, j: (0, j))
    out, n0a, n0b, n1a, n1b = pl.pallas_call(
        functools.partial(_ffn_body, sh=db, t_s=t_s, has_state=True, tiles_per_seq=1),
        grid=(1, nj),
        in_specs=[pl.BlockSpec((tm, d), lambda i, j: (0, 0)), vec,
                  pl.BlockSpec((d, tn), lambda i, j: (0, j)), pl.BlockSpec((d, tn), lambda i, j: (0, j + nj)),
                  pl.BlockSpec((3, 2, tn), lambda i, j: (0, 0, j)), pl.BlockSpec((2, tn), lambda i, j: (0, j)),
                  pl.BlockSpec((tn, d), lambda i, j: (j, 0)), vec, *st_specs],
        out_specs=[pl.BlockSpec((db, t_s * d), lambda i, j: (0, 0)), nf_spec, nf_spec, nf_spec, nf_spec],
        out_shape=[jax.ShapeDtypeStruct((db, t_s * d), F32)] + [jax.ShapeDtypeStruct((db, dff), F32)] * 4,
        scratch_shapes=[pltpu.VMEM((tm, d), BF16),
                        *[pltpu.VMEM((2 * db + tm, sb), F32)] * (2 * nsb), *[pltpu.VMEM((tm, sb), BF16)] * nsb],
        compiler_params=_cparams(2, 56),
        name="ffn_sample",
    )(x_tm, g_pre, w_up, w_up, cw3, cb2, w_down, g_post, st2d, st2d, st2d, st2d)
    new_ffn = jnp.stack([jnp.concatenate([n0a, n0b], axis=1), jnp.concatenate([n1a, n1b], axis=1)], axis=1)
    return out, new_ffn


def _pad_cols(x, width):
    return jnp.pad(x, [(0, 0)] * (x.ndim - 1) + [(0, width - x.shape[-1])])


def _round_up(n, m):
    return -(-n // m) * m


def kernel(x_prompt, x_sample, cache_mem_k, cache_mem_v, state_conv, state_shift, state_wkv, state_ffn, mem_prompt, norm_mix_pre, w_in, conv_dw, conv_dw_b, conv_ln_g, conv_ln_b, rwkv_mu, w0, w_lora, a0, a_lora, g_lora, k_k, k_a, r_k, ln_x_g, ln_x_b, w_out, norm_mix_post, norm_xa_pre, norm_mem, w_q, w_k, w_v, w_o, norm_xa_post, norm_ffn_pre, w_up, ffn_dw, ffn_dw_b, w_down, norm_ffn_post):
    nb, t, d = x_prompt.shape
    db, t_s, _ = x_sample.shape
    depth = w_in.shape[0]
    c = conv_dw.shape[-1]
    w = w0.shape[-1]
    n_heads, hn = state_wkv.shape[2], state_wkv.shape[3]
    dl, al, gl = w_lora.shape[1], a_lora.shape[1], g_lora.shape[1]
    dlp, alp, glp = _round_up(dl, LANES), _round_up(al, LANES), _round_up(gl, LANES)
    lp = dlp + alp + glp
    n_mem, xa_heads, xa_hd = cache_mem_k.shape[2:]
    xd = xa_heads * xa_hd
    dff = w_down.shape[1]
    kc = conv_dw.shape[1]
    assert depth == 1 and c == w and (2 * c + 3 * w) % lp == 0 and (3 * w) % lp == 0 and hn * 2 == LANES

    def pad_rcols(x):
        o = 3 * w
        return jnp.concatenate([x[..., :o], _pad_cols(x[..., o:o + dl], dlp), _pad_cols(x[..., o + dl:o + dl + al], alp),
                                _pad_cols(x[..., o + dl + al:], glp)], axis=-1)

    def unpad_rcols(x):
        o = 3 * w
        return jnp.concatenate([x[..., :o + dl], x[..., o + dlp:o + dlp + al], x[..., o + dlp + alp:o + dlp + alp + gl]], axis=-1)

    row = lambda v: v.reshape(1, -1)
    l = 0
    w_in_p = jnp.concatenate([w_in[l][:, :2 * c], pad_rcols(w_in[l][:, 2 * c:])], axis=1).astype(BF16)
    mu_p = pad_rcols(rwkv_mu[l]).reshape(1, -1)
    wl_p = jnp.pad(w_lora[l], ((0, dlp - dl), (0, 0))).astype(BF16)
    al_p = jnp.pad(a_lora[l], ((0, alp - al), (0, 0))).astype(BF16)
    gl_p = jnp.pad(g_lora[l], ((0, glp - gl), (0, 0))).astype(BF16)
    w_out_b, w_q_b, w_k_b, w_v_b, w_o_b = (x[l].astype(BF16) for x in (w_out, w_q, w_k, w_v, w_o))
    w_up_b, w_down_b = w_up[l].astype(BF16), w_down[l].astype(BF16)
    cw3 = ffn_dw[l].reshape(ffn_dw.shape[1], 2, dff)
    cb2 = ffn_dw_b[l].reshape(2, dff)
    g_mix_pre, g_mix_post, g_xa_pre, g_mem, g_xa_post, g_ffn_pre, g_ffn_post = (
        row(x[l]) for x in (norm_mix_pre, norm_mix_post, norm_xa_pre, norm_mem, norm_xa_post, norm_ffn_pre, norm_ffn_post))
    prep_params = (row(w0[l]), row(a0[l]), row(k_k[l]), row(k_a[l]), wl_p, al_p, gl_p)
    wkv_params = (row(r_k[l]), row(ln_x_g[l]), row(ln_x_b[l]))
    conv_params = (conv_dw[l], row(conv_dw_b[l]), row(conv_ln_g[l]), row(conv_ln_b[l]))

    tl = _TILES
    tm_a = min(tl["tm_a"], nb * t)
    tn_a = tl["tn_a"]
    tm_e = min(tl["tm_e"], nb * t)
    tk_e = tl["tk_e"]
    chunk = min(tl["chunk"], t)
    pp = min(tl["pp"], w // LANES)
    tn_f = min(tl["tn_f"], dff)

    n_mem_rows = nb * n_mem
    mem2d = mem_prompt.reshape(n_mem_rows, d)
    mk = _norm_matmul(mem2d, g_mem, w_k_b, tm=min(tl["tm_a"], n_mem_rows), tn=tn_a, out_dtype=F32, name="mem_k")
    mv = _norm_matmul(mem2d, g_mem, w_v_b, tm=min(tl["tm_a"], n_mem_rows), tn=tn_a, out_dtype=F32, name="mem_v")
    xp = x_prompt.reshape(nb * t, d)
    proj = _norm_matmul(xp, g_mix_pre, w_in_p, tm=tm_a, tn=tn_a, out_dtype=F32, name="proj_prompt")
    cv, conv_p = _conv_prompt(proj, nb, t, c, *conv_params, tt=min(tl["tt_conv"], t))
    tt_p = min(tl["tt_prep"], t)
    streams = _prep(proj, None, mu_p, *prep_params, n_seq_tiles=nb, tiles_per_seq=t // tt_p, tt=tt_p, c=c, w=w, lp=lp, sh=1, hn=hn)
    rw, wkv_p = _wkv(streams, *wkv_params, None, nb=nb, t=t, rows=chunk, w=w, hn=hn, pp=pp, out_dtype=BF16)
    x1 = _mm_norm_res([cv, rw], [w_out_b[:c], w_out_b[c:]], g_mix_post, xp, tm=tm_e, tk=tk_e, name="mix_out_prompt")
    q = _norm_matmul(x1, g_xa_pre, w_q_b, tm=tm_a, tn=tn_a, out_dtype=BF16, name="q_prompt")
    o = _attn_prompt(q, mk.reshape(nb, n_mem, xd), mv.reshape(nb, n_mem, xd), nb, t, xa_heads, tq=min(tl["tq"], t))
    x2 = _mm_norm_res([o], [w_o_b], g_xa_post, x1, tm=tm_e, tk=tk_e, name="attn_out_prompt")
    yp, ffn_p = _ffn_prompt(x2, g_ffn_pre, w_up_b, cw3, cb2, w_down_b, g_ffn_post, nb=nb, t=t, tm=min(tl["tm_f"], t), tn=tn_f)
    shift_p = unpad_rcols(proj.reshape(nb, t, -1)[:, -1, 2 * c:])

    rows_s = t_s * db
    xs_bm = x_sample.reshape(db, t_s * d)
    proj_s = _norm_matmul(xs_bm, g_mix_pre, w_in_p, tm=rows_s, tn=tn_a, out_dtype=F32, name="proj_sample")
    cv_s, conv_s = _conv_sample(proj_s, state_conv[l].reshape(db, (kc - 1) * c), db, t_s, c, *conv_params)
    streams_s = _prep(proj_s, pad_rcols(state_shift[l]), mu_p, *prep_params, n_seq_tiles=1, tiles_per_seq=t_s, tt=db,
                      c=c, w=w, lp=lp, sh=db, hn=hn, out_batch_major=True)
    streams_s = [x.reshape(rows_s, w) for x in streams_s]
    rw_s, wkv_s = _wkv(streams_s, *wkv_params, state_wkv[l].reshape(db, w // LANES, LANES, hn), nb=db, t=t_s,
                       rows=tl["chunk"], w=w, hn=hn, pp=pp, out_dtype=F32)
    x1_s = _mm_norm_res([cv_s, rw_s.reshape(db, t_s * w)], [w_out_b[:c], w_out_b[c:]], g_mix_post, xs_bm, tm=db, tk=tk_e,
                        name="mix_out_sample", a_batch_major=(False, True), res_batch_major=True)
    q_s = _norm_matmul(x1_s, g_xa_pre, w_q_b, tm=db, tn=tn_a, out_dtype=F32, name="q_sample", out_batch_major=True)
    o_s = _attn_sample(q_s.reshape(rows_s, xd), cache_mem_k, cache_mem_v, db, t_s)
    x2_s = _mm_norm_res([o_s.reshape(db, t_s * xd)], [w_o_b], g_xa_post, x1_s, tm=db, tk=tk_e, name="attn_out_sample",
                        a_batch_major=(True,))
    ys2, ffn_s = _ffn_sample(x2_s, state_ffn[l].reshape(db, 4 * dff), g_ffn_pre, w_up_b, cw3, cb2, w_down_b, g_ffn_post,
                             db=db, t_s=t_s, tn=tn_f)
    shift_s = unpad_rcols(proj_s[(t_s - 1) * db:, 2 * c:])

    return (yp.reshape(nb, t, d), ys2.reshape(db, t_s, d),
            conv_p[None], conv_s.reshape(db, kc - 1, c)[None],
            shift_p[None], shift_s[None],
            wkv_p.reshape(nb, n_heads, hn, hn)[None], wkv_s.reshape(db, n_heads, hn, hn)[None],
            ffn_p[None], ffn_s[None],
            mk.reshape(nb, n_mem, xa_heads, xa_hd)[None], mv.reshape(nb, n_mem, xa_heads, xa_hd)[None])
```

```python
import functools
import math

import jax
import jax.numpy as jnp
from jax import lax
from jax.experimental import pallas as pl
from jax.experimental.pallas import tpu as pltpu

F32 = jnp.float32
BF16 = jnp.bfloat16
RMS_EPS = 1e-6
LN_EPS = 1e-5
GN_EPS = 64e-5
LANES = 128
ROW_CHUNK = 16
NORM_UNROLL = 4
FFN_SUB = 256
DECAY_SCALE = math.exp(-0.5)
_NT = (((1,), (1,)), ((), ()))


_TILES = dict(
    tm_a=1024, tn_a=512,
    tm_e=512, tk_e=2048,
    tt_conv=256, tt_prep=256,
    chunk=64, pp=8,
    tq=256,
    tm_f=512, tn_f=512,
)


def _cparams(n_grid, vmem_mib=48):
    return pltpu.CompilerParams(dimension_semantics=("arbitrary",) * n_grid,
                                vmem_limit_bytes=vmem_mib * 1024 * 1024)


def _sigmoid(x):
    return 1.0 / (1.0 + jnp.exp(-x))


def _rms(x, g):
    return x * lax.rsqrt(jnp.mean(x * x, axis=-1, keepdims=True) + RMS_EPS) * g


def _row_loop(n_rows, rc, fn, unroll=1):
    def body(i, carry):
        fn(pl.multiple_of(i * rc, rc))
        return carry
    lax.fori_loop(0, n_rows // rc, body, 0, unroll=unroll)


def _res_norm_rows(n_rows, g, load_y, load_res, store):
    n_grp = math.gcd(NORM_UNROLL, n_rows // ROW_CHUNK)

    def group(r0):
        rs = [pl.multiple_of(r0 + k * ROW_CHUNK, ROW_CHUNK) for k in range(n_grp)]
        ys = [load_y(r) for r in rs]
        res = [load_res(r) for r in rs]
        outs = [x + _rms(y, g) for x, y in zip(res, ys)]
        for r, o in zip(rs, outs):
            store(r, o)
    _row_loop(n_rows, ROW_CHUNK * n_grp, group)


def _mm(x, y, precision=None):
    return jnp.dot(x, y, precision=precision, preferred_element_type=F32)


def _norm_rows_to(h_scr, x_ref, g_ref):
    g = g_ref[...]
    rows = x_ref.shape[0]
    d = h_scr.shape[1]
    for t in range(x_ref.shape[1] // d):
        def chunk(r0, t=t):
            dst = pl.multiple_of(t * rows + r0, ROW_CHUNK)
            h_scr[pl.ds(dst, ROW_CHUNK), :] = _rms(x_ref[pl.ds(r0, ROW_CHUNK), t * d:(t + 1) * d], g).astype(BF16)
        _row_loop(rows, ROW_CHUNK, chunk, unroll=NORM_UNROLL)


def _norm_matmul_body(x_ref, g_ref, w_ref, o_ref, h_scr):
    @pl.when(pl.program_id(1) == 0)
    def _():
        _norm_rows_to(h_scr, x_ref, g_ref)

    o_ref[...] = _mm(h_scr[...], w_ref[...]).astype(o_ref.dtype)


def _norm_matmul(x, g, w, *, tm, tn, out_dtype, name, out_batch_major=False):
    d, n = w.shape
    n_t = x.shape[1] // d
    m = x.shape[0] * n_t
    assert n_t == 1 or m == tm
    tn = min(tn, n)
    nj = n // tn
    if out_batch_major:
        out_spec = pl.BlockSpec((tm, tn), lambda i, j: (0, i * nj + j))
        out_shape = jax.ShapeDtypeStruct((tm, (m // tm) * n), out_dtype)
    else:
        out_spec = pl.BlockSpec((tm, tn), lambda i, j: (i, j))
        out_shape = jax.ShapeDtypeStruct((m, n), out_dtype)
    return pl.pallas_call(
        _norm_matmul_body,
        grid=(m // tm, nj),
        in_specs=[pl.BlockSpec((tm // n_t, n_t * d), lambda i, j: (i, 0)),
                  pl.BlockSpec((1, d), lambda i, j: (0, 0)), pl.BlockSpec((d, tn), lambda i, j: (0, j))],
        out_specs=out_spec,
        out_shape=out_shape,
        scratch_shapes=[pltpu.VMEM((tm, d), BF16)],
        compiler_params=_cparams(2),
        name=name,
    )(x, g, w)


def _mm_norm_res_body(*refs, n_a):
    a_refs = refs[:n_a]
    w_refs = refs[n_a:2 * n_a]
    g_ref, r_ref, o_ref = refs[2 * n_a:2 * n_a + 3]
    k = pl.program_id(1)

    part = None
    for a, w in zip(a_refs, w_refs):
        d = _mm(a[...].astype(BF16), w[...])
        part = d if part is None else part + d

    @pl.when(k == 0)
    def _():
        o_ref[...] = part

    @pl.when(k > 0)
    def _():
        o_ref[...] += part

    @pl.when(k == pl.num_programs(1) - 1)
    def _():
        def store(r, v):
            o_ref[pl.ds(r, ROW_CHUNK), :] = v
        _res_norm_rows(o_ref.shape[0], g_ref[...], lambda r: o_ref[pl.ds(r, ROW_CHUNK), :],
                       lambda r: r_ref[pl.ds(r, ROW_CHUNK), :], store)


def _mm_norm_res(a_list, w_list, g, res, *, tm, tk, name, a_batch_major=(), res_batch_major=False):
    kdim, n = w_list[0].shape
    n_a = len(a_list)
    m = res.size // n
    tk = min(tk, kdim)
    nk = kdim // tk
    a_specs = []
    for idx in range(n_a):
        if idx < len(a_batch_major) and a_batch_major[idx]:
            a_specs.append(pl.BlockSpec((tm, tk), lambda i, k: (0, i * nk + k)))
        else:
            a_specs.append(pl.BlockSpec((tm, tk), lambda i, k: (i, k)))
    return pl.pallas_call(
        functools.partial(_mm_norm_res_body, n_a=n_a),
        grid=(m // tm, nk),
        in_specs=[*a_specs,
                  *[pl.BlockSpec((tk, n), lambda i, k: (k, 0)) for _ in w_list],
                  pl.BlockSpec((1, n), lambda i, k: (0, 0)),
                  pl.BlockSpec((tm, n), (lambda i, k: (0, i)) if res_batch_major else (lambda i, k: (i, 0)))],
        out_specs=pl.BlockSpec((tm, n), lambda i, k: (i, 0)),
        out_shape=jax.ShapeDtypeStruct((m, n), F32),
        compiler_params=_cparams(2),
        name=name,
    )(*a_list, *w_list, g, res)


def _ln_swish(y, lg, lb):
    mu = jnp.mean(y, axis=-1, keepdims=True)
    d = y - mu
    var = jnp.mean(d * d, axis=-1, keepdims=True)
    yn = d * lax.rsqrt(var + LN_EPS) * lg + lb
    return yn * _sigmoid(yn)


def _conv_prompt_body(a_ref, b_ref, w_ref, cb_ref, lg_ref, lb_ref, cv_ref, nc_ref, full, cvt, *, kw, lane_blk):
    tt, c = a_ref.shape
    past = kw - 1
    hp = -(-past // 8) * 8
    lead = hp - past
    ti = pl.program_id(1)

    @pl.when(ti == 0)
    def _():
        full[0:hp, :] = jnp.zeros((hp, c), F32)

    @pl.when(ti > 0)
    def _():
        full[0:hp, :] = full[tt:tt + hp, :]

    def glu(r0):
        dst = pl.multiple_of(hp + r0, ROW_CHUNK)
        full[pl.ds(dst, ROW_CHUNK), :] = a_ref[pl.ds(r0, ROW_CHUNK), :] * _sigmoid(b_ref[pl.ds(r0, ROW_CHUNK), :])
    _row_loop(tt, ROW_CHUNK, glu)

    rt = 32

    def taps(r0):
        for lb in range(c // lane_blk):
            ls = slice(lb * lane_blk, (lb + 1) * lane_blk)
            win = full[pl.ds(r0, rt + hp), ls]
            acc = jnp.broadcast_to(cb_ref[:, ls], (rt, lane_blk))
            for s in range(8):
                offs = [o for o in range(lead, lead + kw) if o % 8 == s]
                if not offs:
                    continue
                span = offs[-1] - s + rt
                shifted = win if s == 0 else win[s:s + span, :]
                for o in offs:
                    acc = acc + shifted[o - s:o - s + rt, :] * w_ref[o - lead:o - lead + 1, ls]
            cvt[pl.ds(r0, rt), ls] = acc
    _row_loop(tt, rt, taps)

    def ln(r0):
        cv_ref[pl.ds(r0, ROW_CHUNK), :] = _ln_swish(cvt[pl.ds(r0, ROW_CHUNK), :], lg_ref[...], lb_ref[...]).astype(cv_ref.dtype)
    _row_loop(tt, ROW_CHUNK, ln, unroll=2)

    @pl.when(ti == pl.num_programs(1) - 1)
    def _():
        nc_ref[...] = full[tt + lead:tt + hp, :]


def _conv_prompt(proj, nb, t, c, conv_w, conv_b, ln_g, ln_b, tt):
    kw = conv_w.shape[0]
    nt = t // tt
    hp = -(-(kw - 1) // 8) * 8
    vec = pl.BlockSpec((1, c), lambda b, i: (0, 0))
    return pl.pallas_call(
        functools.partial(_conv_prompt_body, kw=kw, lane_blk=LANES),
        grid=(nb, nt),
        in_specs=[pl.BlockSpec((tt, c), lambda b, i: (b * nt + i, 0)),
                  pl.BlockSpec((tt, c), lambda b, i: (b * nt + i, 1)),
                  pl.BlockSpec((kw, c), lambda b, i: (0, 0)), vec, vec, vec],
        out_specs=[pl.BlockSpec((tt, c), lambda b, i: (b * nt + i, 0)),
                   pl.BlockSpec((None, kw - 1, c), lambda b, i: (b, 0, 0))],
        out_shape=[jax.ShapeDtypeStruct((nb * t, c), BF16), jax.ShapeDtypeStruct((nb, kw - 1, c), F32)],
        scratch_shapes=[pltpu.VMEM((tt + hp, c), F32), pltpu.VMEM((tt, c), F32)],
        compiler_params=_cparams(2),
        name="conv_prompt",
    )(proj, proj, conv_w, conv_b, ln_g, ln_b)


def _conv_sample_body(a_ref, b_ref, st_ref, w_ref, cb_ref, lg_ref, lb_ref, cv_ref, nc_ref, glu, cvt, *, t_s, kw):
    db = st_ref.shape[0]
    c = w_ref.shape[1]
    past = kw - 1
    rows = t_s * db

    def make_glu(r0):
        glu[pl.ds(r0, ROW_CHUNK), :] = a_ref[pl.ds(r0, ROW_CHUNK), :] * _sigmoid(b_ref[pl.ds(r0, ROW_CHUNK), :])
    _row_loop(rows, ROW_CHUNK, make_glu)

    def taps(r0):
        for lb in range(c // LANES):
            ls = slice(lb * LANES, (lb + 1) * LANES)
            srcs = []
            for f in range(past + t_s):
                if f < past:
                    srcs.append(st_ref[pl.ds(r0, 8), f * c + ls.start:f * c + ls.stop])
                else:
                    srcs.append(glu[pl.ds(pl.multiple_of((f - past) * db + r0, 8), 8), ls])
            for t in range(t_s):
                acc = jnp.broadcast_to(cb_ref[:, ls], (8, LANES))
                for j in range(kw):
                    acc = acc + srcs[t + j] * w_ref[j:j + 1, ls]
                cvt[pl.ds(pl.multiple_of(t * db + r0, 8), 8), ls] = acc
    _row_loop(db, 8, taps)

    def ln(r0):
        cv_ref[pl.ds(r0, ROW_CHUNK), :] = _ln_swish(cvt[pl.ds(r0, ROW_CHUNK), :], lg_ref[...], lb_ref[...]).astype(cv_ref.dtype)
    _row_loop(rows, ROW_CHUNK, ln)

    for r in range(past):
        f = t_s + r
        if f < past:
            nc_ref[:, r * c:(r + 1) * c] = st_ref[:, f * c:(f + 1) * c]
        else:
            nc_ref[:, r * c:(r + 1) * c] = glu[(f - past) * db:(f - past + 1) * db, :]


def _conv_sample(proj, state2d, db, t_s, c, conv_w, conv_b, ln_g, ln_b):
    kw = conv_w.shape[0]
    rows = t_s * db
    vec = pl.BlockSpec((1, c), lambda i: (0, 0))
    return pl.pallas_call(
        functools.partial(_conv_sample_body, t_s=t_s, kw=kw),
        grid=(1,),
        in_specs=[pl.BlockSpec((rows, c), lambda i: (0, 0)),
                  pl.BlockSpec((rows, c), lambda i: (0, 1)),
                  pl.BlockSpec((db, (kw - 1) * c), lambda i: (0, 0)),
                  pl.BlockSpec((kw, c), lambda i: (0, 0)), vec, vec, vec],
        out_specs=[pl.BlockSpec((rows, c), lambda i: (0, 0)),
                   pl.BlockSpec((db, (kw - 1) * c), lambda i: (0, 0))],
        out_shape=[jax.ShapeDtypeStruct((rows, c), BF16), jax.ShapeDtypeStruct((db, (kw - 1) * c), F32)],
        scratch_shapes=[pltpu.VMEM((rows, c), F32), pltpu.VMEM((rows, c), F32)],
        compiler_params=_cparams(1, 56),
        name="conv_sample",
    )(proj, proj, state2d, conv_w, conv_b, ln_g, ln_b)


def _head_ones(hn):
    ri = lax.broadcasted_iota(jnp.int32, (LANES, LANES), 0)
    ci = lax.broadcasted_iota(jnp.int32, (LANES, LANES), 1)
    return (ri // hn) == (ci // hn)


def _prep_body(*refs, sh, has_state, hn, dlp, alp):
    pr = refs[0:4]
    i0 = 4
    st = None
    if has_state:
        st = refs[i0:i0 + 4]
        i0 += 4
    mu = refs[i0:i0 + 4]
    w0_ref, a0_ref, kk_ref, ka_ref, wl_ref, al_ref, gl_ref = refs[i0 + 4:i0 + 11]
    r_ref, lw_ref, k2_ref, v_ref, a_ref, b_ref, g_ref = refs[i0 + 11:i0 + 18]
    bufs = refs[i0 + 18:i0 + 22]
    lact = refs[i0 + 22]
    tt, w = r_ref.shape
    off = -(-sh // 8) * 8
    ti = pl.program_id(1)

    @pl.when(ti == 0)
    def _():
        for n, bf in enumerate(bufs):
            bf[0:off, :] = st[n][...] if has_state else jnp.zeros((off, bf.shape[1]), F32)

    @pl.when(ti > 0)
    def _():
        for bf in bufs:
            bf[0:off, :] = bf[tt:tt + off, :]

    def copy(r0):
        dst = pl.multiple_of(off + r0, 8)
        for bf, p in zip(bufs, pr):
            bf[pl.ds(dst, 8), :] = p[pl.ds(r0, 8), :]
    _row_loop(tt, 8, copy)

    def shifted(bf, m, r0):
        cur = bf[pl.ds(pl.multiple_of(off + r0, 8), 8), :]
        if sh % 8 == 0:
            prev = bf[pl.ds(pl.multiple_of(off - sh + r0, 8), 8), :]
        else:
            win = bf[pl.ds(r0, 16), :]
            prev = win[8 - sh:16 - sh, :]
        return cur + (prev - cur) * m[...]

    def stage1(r0):
        r_ref[pl.ds(r0, 8), :] = shifted(bufs[0], mu[0], r0)
        k = shifted(bufs[1], mu[1], r0)
        k2_ref[pl.ds(r0, 8), :] = k
        kk = k * kk_ref[...]
        b_ref[pl.ds(r0, 8), :] = kk * kk
        v_ref[pl.ds(r0, 8), :] = shifted(bufs[2], mu[2], r0)
    _row_loop(tt, 8, stage1)

    def stage1l(r0):
        xl = None
        for h in range(2):
            part = shifted(bufs[3], mu[3], r0 + 8 * h)
            xl = part if xl is None else jnp.concatenate([xl, part], axis=0)
        lane = lax.broadcasted_iota(jnp.int32, xl.shape, 1)
        act = jnp.where(lane < dlp, jnp.tanh(xl), jnp.where(lane < dlp + alp, xl, _sigmoid(xl)))
        lact[pl.ds(r0, ROW_CHUNK), :] = act.astype(BF16)
    _row_loop(tt, ROW_CHUNK, stage1l)

    lw_ref[...] = _mm(lact[:, 0:dlp], wl_ref[...])
    a_ref[...] = _mm(lact[:, dlp:dlp + alp], al_ref[...])
    g_ref[...] = _mm(lact[:, dlp + alp:], gl_ref[...])
    ones = jnp.where(_head_ones(hn), 1.0, 0.0).astype(BF16)
    for q in range(w // LANES):
        ls = slice(q * LANES, (q + 1) * LANES)
        b_ref[:, ls] = _dot_exact_rhs(_split3(b_ref[:, ls]), ones)

    def stage3(r0):
        rs = pl.ds(r0, 8)
        k = k2_ref[rs, :]
        lw_ref[rs, :] = -DECAY_SCALE * _sigmoid(w0_ref[...] + lw_ref[rs, :])
        asig = _sigmoid(a0_ref[...] + a_ref[rs, :])
        kk = k * kk_ref[...]
        kkn = kk / jnp.maximum(jnp.sqrt(b_ref[rs, :]), 1e-12)
        k2_ref[rs, :] = k * (1.0 + (asig - 1.0) * ka_ref[...])
        a_ref[rs, :] = -kkn
        b_ref[rs, :] = kkn * asig
    _row_loop(tt, 8, stage3)


def _prep(proj, st, mu_p, w0, a0, k_k, k_a, wl, al, gl, *, n_seq_tiles, tiles_per_seq, tt, c, w, lp, sh, hn,
          out_batch_major=False):
    dlp, alp = wl.shape[0], al.shape[0]
    has_state = st is not None
    rows = n_seq_tiles * tiles_per_seq * tt
    cb = 2 * c // w
    lb = (2 * c + 3 * w) // lp
    off = -(-sh // 8) * 8
    rmap = lambda o: (lambda s, i: (s * tiles_per_seq + i, o))
    in_specs = [pl.BlockSpec((tt, w), rmap(cb)), pl.BlockSpec((tt, w), rmap(cb + 1)),
                pl.BlockSpec((tt, w), rmap(cb + 2)), pl.BlockSpec((tt, lp), rmap(lb))]
    args = [proj, proj, proj, proj]
    if has_state:
        in_specs += [pl.BlockSpec((sh, w), lambda s, i: (0, 0)), pl.BlockSpec((sh, w), lambda s, i: (0, 1)),
                     pl.BlockSpec((sh, w), lambda s, i: (0, 2)), pl.BlockSpec((sh, lp), lambda s, i: (0, 3 * w // lp))]
        args += [st, st, st, st]
    in_specs += [pl.BlockSpec((1, w), lambda s, i: (0, 0)), pl.BlockSpec((1, w), lambda s, i: (0, 1)),
                 pl.BlockSpec((1, w), lambda s, i: (0, 2)), pl.BlockSpec((1, lp), lambda s, i: (0, 3 * w // lp))]
    args += [mu_p, mu_p, mu_p, mu_p]
    vec = pl.BlockSpec((1, w), lambda s, i: (0, 0))
    in_specs += [vec, vec, vec, vec,
                 pl.BlockSpec(wl.shape, lambda s, i: (0, 0)), pl.BlockSpec(al.shape, lambda s, i: (0, 0)),
                 pl.BlockSpec(gl.shape, lambda s, i: (0, 0))]
    args += [w0, a0, k_k, k_a, wl, al, gl]
    if out_batch_major:
        assert n_seq_tiles == 1
        ospec = pl.BlockSpec((tt, w), lambda s, i: (0, i))
        oshape = jax.ShapeDtypeStruct((tt, tiles_per_seq * w), F32)
    else:
        ospec = pl.BlockSpec((tt, w), lambda s, i: (s * tiles_per_seq + i, 0))
        oshape = jax.ShapeDtypeStruct((rows, w), F32)
    return pl.pallas_call(
        functools.partial(_prep_body, sh=sh, has_state=has_state, hn=hn, dlp=dlp, alp=alp),
        grid=(n_seq_tiles, tiles_per_seq),
        in_specs=in_specs,
        out_specs=[ospec] * 7,
        out_shape=[oshape] * 7,
        scratch_shapes=[pltpu.VMEM((off + tt, w), F32)] * 3 + [pltpu.VMEM((off + tt, lp), F32), pltpu.VMEM((tt, lp), BF16)],
        compiler_params=_cparams(2, 56),
        name="rwkv_prep_state" if has_state else "rwkv_prep",
    )(*args)


def _split2(x):
    hi = x.astype(BF16)
    lo = (x - hi.astype(F32)).astype(BF16)
    return hi, lo


def _split3(x):
    hi = x.astype(BF16)
    r1 = x - hi.astype(F32)
    mid = r1.astype(BF16)
    lo = (r1 - mid.astype(F32)).astype(BF16)
    return hi, mid, lo


def _dot3(xs, ys, dims=(((1,), (0,)), ((), ()))):
    def d(a, b):
        return lax.dot_general(a, b, dims, preferred_element_type=F32)
    return d(xs[0], ys[0]) + (d(xs[0], ys[1]) + d(xs[1], ys[0]))


def _dot_exact_rhs(parts, y_bf16):
    out = None
    for p in parts[::-1]:
        d = _mm(p, y_bf16)
        out = d if out is None else out + d
    return out


def _wkv_body(*refs, sub, pp, has_state, hn):
    r_ref, lw_ref, k_ref, v_ref, a_ref, b_ref, g_ref, rk_ref, lg_ref, lb_ref = refs[:10]
    i0 = 10
    s0_ref = None
    if has_state:
        s0_ref = refs[i0]
        i0 += 1
    o_ref, so_ref, sbd = refs[i0:i0 + 3]
    ell = r_ref.shape[0]
    l2 = 2 * ell
    nseq = ell // sub
    n_iter = max(1, int(math.log2(sub)))
    c = pl.program_id(2)
    bd = _head_ones(hn)
    lane = lax.broadcasted_iota(jnp.int32, (1, LANES), 1)
    lo = lane < hn
    row_lo = lax.broadcasted_iota(jnp.int32, (LANES, LANES), 0) < hn

    @pl.when(c == 0)
    def _():
        if has_state:
            for s in range(nseq):
                for q in range(pp):
                    sbd[s * pp + q] = jnp.zeros((LANES, LANES), F32)
                    sbd[s * pp + q, :, 0:hn] = s0_ref[s, q]
                    x = sbd[s * pp + q]
                    sbd[s * pp + q] = jnp.where(bd, x + pltpu.roll(x, hn, 1), 0.0)
        else:
            sbd[...] = jnp.zeros(sbd.shape, F32)

    lw = lw_ref[...]
    ri = lax.broadcasted_iota(jnp.int32, (ell, ell), 0)
    ci = lax.broadcasted_iota(jnp.int32, (ell, ell), 1)
    tri = jnp.where((ri >= ci) & ((ri // sub) == (ci // sub)), 1.0, 0.0).astype(BF16)
    cs = None
    for part in _split3(lw)[::-1]:
        d = _mm(tri, part)
        cs = d if cs is None else cs + d
    w_in = jnp.exp(cs)
    w_inv = jnp.exp(-cs)
    a_t = a_ref[...] * jnp.exp(cs - lw)
    r_t = r_ref[...] * w_in
    b_t = b_ref[...] * w_inv
    k_t = k_ref[...] * w_inv
    rr = lax.broadcasted_iota(jnp.int32, (l2, l2), 0) % ell
    cc = lax.broadcasted_iota(jnp.int32, (l2, l2), 1) % ell
    same_seq = (rr // sub) == (cc // sub)
    strict = (cc < rr) & same_seq
    incl = (cc <= rr) & same_seq
    ones = jnp.where(bd, 1.0, 0.0).astype(BF16)
    avg = jnp.where(bd, 1.0 / hn, 0.0).astype(BF16)

    def seq_rows(x, s):
        return jnp.concatenate([x[s * sub:(s + 1) * sub], x[ell + s * sub:ell + (s + 1) * sub]], axis=0)

    def head_stack(pieces):
        return jnp.concatenate([p[0:sub] for p in pieces] + [p[sub:2 * sub] for p in pieces], axis=0)

    pairs = range(pp)
    lss = [slice(q * LANES, (q + 1) * LANES) for q in pairs]

    def blk(x, q):
        xs = x[:, lss[q]]
        return jnp.concatenate([jnp.where(lo, xs, 0.0), jnp.where(lo, 0.0, xs)], axis=0)

    ab = [blk(a_t, q) for q in pairs]
    rb = [blk(r_t, q) for q in pairs]
    bb = [blk(b_t, q) for q in pairs]
    kb = [blk(k_t, q) for q in pairs]
    vb = [blk(v_ref[...], q) for q in pairs]
    a_s = [_split2(ab[q]) for q in pairs]
    r_hi = [rb[q].astype(BF16) for q in pairs]
    bk_s = [_split2(jnp.concatenate([bb[q], kb[q]], axis=0)) for q in pairs]
    v_s = [_split2(vb[q]) for q in pairs]
    g_a = [_dot3(a_s[q], bk_s[q], _NT) for q in pairs]
    g_r = [lax.dot_general(r_hi[q], bk_s[q][0], _NT, preferred_element_type=F32) for q in pairs]
    m_s = [_split2(jnp.where(strict, g_a[q][:, 0:l2], 0.0)) for q in pairs]
    m_ak = [_split2(jnp.where(strict, g_a[q][:, l2:], 0.0)) for q in pairs]
    m_r = [jnp.concatenate([jnp.where(incl, g_r[q][:, 0:l2], 0.0), jnp.where(incl, g_r[q][:, l2:], 0.0)],
                           axis=1).astype(BF16) for q in pairs]

    ps_a, ps_r = [], []
    for q in pairs:
        if nseq == 1:
            s_s = _split2(sbd[q])
            ps_a.append(_dot3(a_s[q], s_s, _NT))
            ps_r.append(lax.dot_general(r_hi[q], s_s[0], _NT, preferred_element_type=F32))
        else:
            pa, pr = [], []
            for s in range(nseq):
                sel = jnp.concatenate([seq_rows(ab[q], s), seq_rows(rb[q], s)], axis=0)
                ps = _dot3(_split2(sel), _split2(sbd[s * pp + q]), _NT)
                pa.append(ps[0:2 * sub])
                pr.append(ps[2 * sub:])
            ps_a.append(head_stack(pa))
            ps_r.append(head_stack(pr))

    u = [ps_a[q] + _dot3(m_ak[q], v_s[q]) for q in pairs]
    for it in range(n_iter):
        u = [u[q] + _dot_exact_rhs(m_s[q], u[q].astype(BF16)) for q in pairs]
        if it < n_iter - 1:
            m_s = [_split2(_dot_exact_rhs(m_s[q], m_s[q][0])) for q in pairs]
    uv = [jnp.concatenate([u[q], vb[q]], axis=0) for q in pairs]
    yb = [ps_r[q] + _mm(m_r[q], uv[q].astype(BF16)) for q in pairs]
    y = [yb[q][0:ell, :] + yb[q][ell:l2, :] for q in pairs]

    for q in pairs:
        for s in range(nseq):
            if nseq == 1:
                uv_t, bk_sel = uv[q].T, bk_s[q]
            else:
                uv_t = jnp.concatenate([seq_rows(u[q], s), seq_rows(vb[q], s)], axis=0).T
                bk_sel = _split2(jnp.concatenate([seq_rows(bb[q], s), seq_rows(kb[q], s)], axis=0))
            upd = _dot3(_split2(uv_t), bk_sel)
            last = (s + 1) * sub - 1
            sbd[s * pp + q] = (sbd[s * pp + q] + upd) * w_in[last:last + 1, lss[q]]

    mu = [_dot_exact_rhs(_split2(y[q]), avg) for q in pairs]
    d = [y[q] - mu[q] for q in pairs]
    var = [_dot_exact_rhs(_split2(d[q] * d[q]), avg) for q in pairs]
    bonus = [_dot_exact_rhs(_split2(r_ref[:, lss[q]] * k_ref[:, lss[q]] * rk_ref[:, lss[q]]), ones) for q in pairs]
    for q in pairs:
        ls = lss[q]
        yn = d[q] * lax.rsqrt(var[q] + GN_EPS) * lg_ref[:, ls] + lb_ref[:, ls]
        o_ref[:, ls] = ((yn + bonus[q] * v_ref[:, ls]) * g_ref[:, ls]).astype(o_ref.dtype)

    @pl.when(c == pl.num_programs(2) - 1)
    def _():
        for s in range(nseq):
            for q in range(pp):
                x = sbd[s * pp + q]
                so_ref[s, q] = jnp.where(row_lo, x, pltpu.roll(x, hn, 1))[:, 0:hn]


def _wkv(streams, r_k, ln_g, ln_b, s0, *, nb, t, rows, w, hn, pp, out_dtype):
    pw = pp * LANES
    ng = w // pw
    has_state = s0 is not None
    sub = min(t, rows)
    nseq = rows // sub
    n_chunks = t // sub
    assert n_chunks == 1 or nseq == 1
    sspec = pl.BlockSpec((rows, pw), lambda b, g, c: (b * n_chunks + c, g))
    vec = pl.BlockSpec((1, pw), lambda b, g, c: (0, g))
    stspec = pl.BlockSpec((nseq, pp, LANES, hn), lambda b, g, c: (b, g, 0, 0))
    in_specs = [sspec] * 7 + [vec] * 3 + ([stspec] if has_state else [])
    args = list(streams) + [r_k, ln_g, ln_b] + ([s0] if has_state else [])
    out, s_new = pl.pallas_call(
        functools.partial(_wkv_body, sub=sub, pp=pp, has_state=has_state, hn=hn),
        grid=(nb // nseq, ng, n_chunks),
        in_specs=in_specs,
        out_specs=[sspec, stspec],
        out_shape=[jax.ShapeDtypeStruct((nb * t, w), out_dtype), jax.ShapeDtypeStruct((nb, w // LANES, LANES, hn), F32)],
        scratch_shapes=[pltpu.VMEM((nseq * pp, LANES, LANES), F32)],
        compiler_params=_cparams(3),
        name="wkv_state" if has_state else "wkv",
    )(*args)
    return out, s_new


def _attn_body(q_ref, k_ref, v_ref, o_ref, kb_ref, vb_ref, *, nh, scale):
    @pl.when(pl.program_id(1) == 0)
    def _():
        kb_ref[...] = k_ref[...].astype(BF16)
        vb_ref[...] = v_ref[...].astype(BF16)
    hd = q_ref.shape[1] // nh
    for h in range(nh):
        hs = slice(h * hd, (h + 1) * hd)
        kh, vh = kb_ref[:, hs], vb_ref[:, hs]
        s = lax.dot_general(q_ref[:, hs].astype(BF16), kh, _NT, preferred_element_type=F32) * scale
        p = jnp.exp(s - jnp.max(s, axis=-1, keepdims=True))
        p = p / jnp.sum(p, axis=-1, keepdims=True)
        o_ref[:, hs] = _mm(p.astype(BF16), vh).astype(o_ref.dtype)


def _attn_prompt(q, k, v, nb, t, nh, tq):
    nm, xd = k.shape[1], k.shape[2]
    nt = t // tq
    kv = pl.BlockSpec((None, nm, xd), lambda b, i: (b, 0, 0))
    return pl.pallas_call(
        functools.partial(_attn_body, nh=nh, scale=(xd // nh) ** -0.5),
        grid=(nb, nt),
        in_specs=[pl.BlockSpec((tq, xd), lambda b, i: (b * nt + i, 0)), kv, kv],
        out_specs=pl.BlockSpec((tq, xd), lambda b, i: (b * nt + i, 0)),
        out_shape=jax.ShapeDtypeStruct((nb * t, xd), BF16),
        scratch_shapes=[pltpu.VMEM((nm, xd), BF16)] * 2,
        compiler_params=_cparams(2),
        name="attn_prompt",
    )(q, k, v)


def _attn_sample_body(q_ref, k_ref, v_ref, o_ref, *, scale):
    nm, nh, hd = k_ref.shape
    t_s = q_ref.shape[0]
    k2 = k_ref[...].reshape(nm * nh, hd).astype(BF16)
    v2 = v_ref[...].reshape(nm * nh, hd).astype(BF16)
    q4 = jnp.concatenate([q_ref[:, h * hd:(h + 1) * hd] for h in range(nh)], axis=0).astype(BF16)
    s = lax.dot_general(q4, k2, _NT, preferred_element_type=F32) * scale
    row_head = lax.broadcasted_iota(jnp.int32, s.shape, 0) // t_s
    col_head = lax.broadcasted_iota(jnp.int32, s.shape, 1) % nh
    s = jnp.where(row_head == col_head, s, -1e30)
    p = jnp.exp(s - jnp.max(s, axis=-1, keepdims=True))
    p = p / jnp.sum(p, axis=-1, keepdims=True)
    o4 = _mm(p.astype(BF16), v2)
    for h in range(nh):
        o_ref[:, h * hd:(h + 1) * hd] = o4[h * t_s:(h + 1) * t_s, :]


def _attn_sample(q, k, v, db, t_s):
    _, _, nm, nh, hd = k.shape
    xd = nh * hd
    qs = pl.BlockSpec((t_s, xd), lambda b: (b, 0))
    kv = pl.BlockSpec((None, None, nm, nh, hd), lambda b: (0, b, 0, 0, 0))
    return pl.pallas_call(
        functools.partial(_attn_sample_body, scale=hd ** -0.5),
        grid=(db,),
        in_specs=[qs, kv, kv],
        out_specs=qs,
        out_shape=jax.ShapeDtypeStruct((db * t_s, xd), F32),
        compiler_params=_cparams(1),
        name="attn_sample",
    )(q, k, v)


def _ffn_body(*refs, sh, t_s, has_state, tiles_per_seq):
    x_ref, gpre_ref, wua_ref, wub_ref, cw_ref, cb_ref, wd_ref, gpost_ref = refs[:8]
    i0 = 8
    st = None
    if has_state:
        st = refs[i0:i0 + 4]
        i0 += 4
    o_ref = refs[i0]
    i0 += 1
    n_nf = 4 if has_state else 1
    nf_refs = refs[i0:i0 + n_nf]
    i0 += n_nf
    hn_scr = refs[i0]
    nsb = (len(refs) - i0 - 1 - (0 if has_state else 1)) // 3
    fas = refs[i0 + 1:i0 + 1 + nsb]
    fbs = refs[i0 + 1 + nsb:i0 + 1 + 2 * nsb]
    acts = refs[i0 + 1 + 2 * nsb:i0 + 1 + 3 * nsb]
    carry = None if has_state else refs[i0 + 1 + 3 * nsb]
    tm, d = x_ref.shape
    sb = fas[0].shape[1]
    tn = nsb * sb
    off = max(8, 2 * sh)
    n_t = o_ref.shape[1] // d
    rows = tm // n_t
    i = pl.program_id(0)
    j = pl.program_id(1)

    @pl.when(j == 0)
    def _():
        _norm_rows_to(hn_scr, x_ref, gpre_ref)
        o_ref[...] = jnp.zeros(o_ref.shape, F32)

    if has_state:
        for s in range(nsb):
            cs = slice(s * sb, (s + 1) * sb)
            fas[s][0:sh, :] = st[0][:, cs]
            fbs[s][0:sh, :] = st[1][:, cs]
            fas[s][sh:2 * sh, :] = st[2][:, cs]
            fbs[s][sh:2 * sh, :] = st[3][:, cs]
    else:
        first = (i % tiles_per_seq) == 0

        @pl.when(first)
        def _():
            for s in range(nsb):
                fas[s][0:off, :] = jnp.zeros((off, sb), F32)
                fbs[s][0:off, :] = jnp.zeros((off, sb), F32)

        @pl.when(jnp.logical_not(first))
        def _():
            for s in range(nsb):
                fas[s][0:off, :] = carry[j, :, s * sb:(s + 1) * sb]
                fbs[s][0:off, :] = carry[j, :, tn + s * sb:tn + (s + 1) * sb]

    def up(s):
        cs = slice(s * sb, (s + 1) * sb)
        fas[s][off:off + tm, :] = _mm(hn_scr[...], wua_ref[:, cs])
        fbs[s][off:off + tm, :] = _mm(hn_scr[...], wub_ref[:, cs])

    def conv_act(s):
        cs = slice(s * sb, (s + 1) * sb)

        def conv(f, h, r0):
            if sh % 8 == 0:
                s2 = f[off - 2 * sh + r0:off - 2 * sh + r0 + ROW_CHUNK, :]
                s1 = f[off - sh + r0:off - sh + r0 + ROW_CHUNK, :]
                s0 = f[off + r0:off + r0 + ROW_CHUNK, :]
            else:
                win = f[r0:r0 + ROW_CHUNK + 8, :]
                s2 = win[8 - 2 * sh:8 - 2 * sh + ROW_CHUNK, :]
                s1 = win[8 - sh:8 - sh + ROW_CHUNK, :]
                s0 = win[8:8 + ROW_CHUNK, :]
            return (cb_ref[h:h + 1, cs] + s2 * cw_ref[0, h:h + 1, cs] + s1 * cw_ref[1, h:h + 1, cs]
                    + s0 * cw_ref[2, h:h + 1, cs])
        for r0 in range(0, tm, ROW_CHUNK):
            ua = conv(fas[s], 0, r0)
            ub = conv(fbs[s], 1, r0)
            acts[s][r0:r0 + ROW_CHUNK, :] = (ua * _sigmoid(ua) * ub).astype(BF16)

    def down(s):
        part = _mm(acts[s][...], wd_ref[s * sb:(s + 1) * sb, :])
        for t in range(n_t):
            o_ref[:, t * d:(t + 1) * d] += part[t * rows:(t + 1) * rows, :]

    up(0)
    for s in range(1, nsb):
        up(s)
        conv_act(s - 1)
        down(s - 1)
    conv_act(nsb - 1)
    down(nsb - 1)

    if not has_state:
        for s in range(nsb):
            carry[j, :, s * sb:(s + 1) * sb] = fas[s][tm:tm + off, :]
            carry[j, :, tn + s * sb:tn + (s + 1) * sb] = fbs[s][tm:tm + off, :]

    for s in range(nsb):
        cs = slice(s * sb, (s + 1) * sb)
        if has_state:
            for r in range(2):
                src = off + (t_s - 2 + r) * sh
                nf_refs[2 * r][:, cs] = fas[s][src:src + sh, :]
                nf_refs[2 * r + 1][:, cs] = fbs[s][src:src + sh, :]
        else:
            nf = nf_refs[0]
            for r in range(2):
                src = off + tm - 2 + r
                nf[r, 0:1, cs] = fas[s][src:src + 1, :]
                nf[r, 1:2, cs] = fbs[s][src:src + 1, :]

    @pl.when(j == pl.num_programs(1) - 1)
    def _():
        for t in range(n_t):
            ts = slice(t * d, (t + 1) * d)

            def store(r, v, ts=ts):
                o_ref[pl.ds(r, ROW_CHUNK), ts] = v
            _res_norm_rows(rows, gpost_ref[...], lambda r, ts=ts: o_ref[pl.ds(r, ROW_CHUNK), ts],
                           lambda r, t=t: x_ref[pl.ds(pl.multiple_of(t * rows + r, ROW_CHUNK), ROW_CHUNK), :], store)


def _ffn_prompt(x, g_pre, w_up, cw3, cb2, w_down, g_post, *, nb, t, tm, tn):
    d = x.shape[1]
    dff = w_down.shape[0]
    nj = dff // tn
    sb = min(FFN_SUB, tn)
    nsb = tn // sb
    tps = t // tm
    vec = pl.BlockSpec((1, d), lambda i, j: (0, 0))
    out, nf = pl.pallas_call(
        functools.partial(_ffn_body, sh=1, t_s=None, has_state=False, tiles_per_seq=tps),
        grid=(nb * tps, nj),
        in_specs=[pl.BlockSpec((tm, d), lambda i, j: (i, 0)), vec,
                  pl.BlockSpec((d, tn), lambda i, j: (0, j)), pl.BlockSpec((d, tn), lambda i, j: (0, j + nj)),
                  pl.BlockSpec((3, 2, tn), lambda i, j: (0, 0, j)), pl.BlockSpec((2, tn), lambda i, j: (0, j)),
                  pl.BlockSpec((tn, d), lambda i, j: (j, 0)), vec],
        out_specs=[pl.BlockSpec((tm, d), lambda i, j: (i, 0)),
                   pl.BlockSpec((None, 2, 2, tn), lambda i, j: (i, 0, 0, j))],
        out_shape=[jax.ShapeDtypeStruct((nb * t, d), F32), jax.ShapeDtypeStruct((nb * tps, 2, 2, dff), F32)],
        scratch_shapes=[pltpu.VMEM((tm, d), BF16),
                        *[pltpu.VMEM((8 + tm, sb), F32)] * (2 * nsb), *[pltpu.VMEM((tm, sb), BF16)] * nsb,
                        pltpu.VMEM((nj, 8, 2 * tn), F32)],
        compiler_params=_cparams(2, 56),
        name="ffn_prompt",
    )(x, g_pre, w_up, w_up, cw3, cb2, w_down, g_post)
    return out, nf.reshape(nb, tps, 2, 2 * dff)[:, -1]


def _ffn_sample(x_tm, st2d, g_pre, w_up, cw3, cb2, w_down, g_post, *, db, t_s, tn):
    d = x_tm.shape[1]
    dff = w_down.shape[0]
    nj = dff // tn
    sb = min(FFN_SUB, tn)
    nsb = tn // sb
    tm = t_s * db
    vec = pl.BlockSpec((1, d), lambda i, j: (0, 0))
    st_specs = [pl.BlockSpec((db, tn), (lambda i, j, o=o: (0, o * nj + j))) for o in range(4)]
    nf_spec = pl.BlockSpec((db, tn), lambda i, j: (0, j))
    out, n0a, n0b, n1a, n1b = pl.pallas_call(
        functools.partial(_ffn_body, sh=db, t_s=t_s, has_state=True, tiles_per_seq=1),
        grid=(1, nj),
        in_specs=[pl.BlockSpec((tm, d), lambda i, j: (0, 0)), vec,
                  pl.BlockSpec((d, tn), lambda i, j: (0, j)), pl.BlockSpec((d, tn), lambda i, j: (0, j + nj)),
                  pl.BlockSpec((3, 2, tn), lambda i, j: (0, 0, j)), pl.BlockSpec((2, tn), lambda i, j: (0, j)),
                  pl.BlockSpec((tn, d), lambda i, j: (j, 0)), vec, *st_specs],
        out_specs=[pl.BlockSpec((db, t_s * d), lambda i, j: (0, 0)), nf_spec, nf_spec, nf_spec, nf_spec],
        out_shape=[jax.ShapeDtypeStruct((db, t_s * d), F32)] + [jax.ShapeDtypeStruct((db, dff), F32)] * 4,
        scratch_shapes=[pltpu.VMEM((tm, d), BF16),
                        *[pltpu.VMEM((2 * db + tm, sb), F32)] * (2 * nsb), *[pltpu.VMEM((tm, sb), BF16)] * nsb],
        compiler_params=_cparams(2, 56),
        name="ffn_sample",
    )(x_tm, g_pre, w_up, w_up, cw3, cb2, w_down, g_post, st2d, st2d, st2d, st2d)
    new_ffn = jnp.stack([jnp.concatenate([n0a, n0b], axis=1), jnp.concatenate([n1a, n1b], axis=1)], axis=1)
    return out, new_ffn


def _pad_cols(x, width):
    return jnp.pad(x, [(0, 0)] * (x.ndim - 1) + [(0, width - x.shape[-1])])


def _round_up(n, m):
    return -(-n // m) * m


def kernel(x_prompt, x_sample, cache_mem_k, cache_mem_v, state_conv, state_shift, state_wkv, state_ffn, mem_prompt, norm_mix_pre, w_in, conv_dw, conv_dw_b, conv_ln_g, conv_ln_b, rwkv_mu, w0, w_lora, a0, a_lora, g_lora, k_k, k_a, r_k, ln_x_g, ln_x_b, w_out, norm_mix_post, norm_xa_pre, norm_mem, w_q, w_k, w_v, w_o, norm_xa_post, norm_ffn_pre, w_up, ffn_dw, ffn_dw_b, w_down, norm_ffn_post):
    nb, t, d = x_prompt.shape
    db, t_s, _ = x_sample.shape
    depth = w_in.shape[0]
    c = conv_dw.shape[-1]
    w = w0.shape[-1]
    n_heads, hn = state_wkv.shape[2], state_wkv.shape[3]
    dl, al, gl = w_lora.shape[1], a_lora.shape[1], g_lora.shape[1]
    dlp, alp, glp = _round_up(dl, LANES), _round_up(al, LANES), _round_up(gl, LANES)
    lp = dlp + alp + glp
    n_mem, xa_heads, xa_hd = cache_mem_k.shape[2:]
    xd = xa_heads * xa_hd
    dff = w_down.shape[1]
    kc = conv_dw.shape[1]
    assert depth == 1 and c == w and (2 * c + 3 * w) % lp == 0 and (3 * w) % lp == 0 and hn * 2 == LANES

    def pad_rcols(x):
        o = 3 * w
        return jnp.concatenate([x[..., :o], _pad_cols(x[..., o:o + dl], dlp), _pad_cols(x[..., o + dl:o + dl + al], alp),
                                _pad_cols(x[..., o + dl + al:], glp)], axis=-1)

    def unpad_rcols(x):
        o = 3 * w
        return jnp.concatenate([x[..., :o + dl], x[..., o + dlp:o + dlp + al], x[..., o + dlp + alp:o + dlp + alp + gl]], axis=-1)

    row = lambda v: v.reshape(1, -1)
    l = 0
    w_in_p = jnp.concatenate([w_in[l][:, :2 * c], pad_rcols(w_in[l][:, 2 * c:])], axis=1).astype(BF16)
    mu_p = pad_rcols(rwkv_mu[l]).reshape(1, -1)
    wl_p = jnp.pad(w_lora[l], ((0, dlp - dl), (0, 0))).astype(BF16)
    al_p = jnp.pad(a_lora[l], ((0, alp - al), (0, 0))).astype(BF16)
    gl_p = jnp.pad(g_lora[l], ((0, glp - gl), (0, 0))).astype(BF16)
    w_out_b, w_q_b, w_k_b, w_v_b, w_o_b = (x[l].astype(BF16) for x in (w_out, w_q, w_k, w_v, w_o))
    w_up_b, w_down_b = w_up[l].astype(BF16), w_down[l].astype(BF16)
    cw3 = ffn_dw[l].reshape(ffn_dw.shape[1], 2, dff)
    cb2 = ffn_dw_b[l].reshape(2, dff)
    g_mix_pre, g_mix_post, g_xa_pre, g_mem, g_xa_post, g_ffn_pre, g_ffn_post = (
        row(x[l]) for x in (norm_mix_pre, norm_mix_post, norm_xa_pre, norm_mem, norm_xa_post, norm_ffn_pre, norm_ffn_post))
    prep_params = (row(w0[l]), row(a0[l]), row(k_k[l]), row(k_a[l]), wl_p, al_p, gl_p)
    wkv_params = (row(r_k[l]), row(ln_x_g[l]), row(ln_x_b[l]))
    conv_params = (conv_dw[l], row(conv_dw_b[l]), row(conv_ln_g[l]), row(conv_ln_b[l]))

    tl = _TILES
    tm_a = min(tl["tm_a"], nb * t)
    tn_a = tl["tn_a"]
    tm_e = min(tl["tm_e"], nb * t)
    tk_e = tl["tk_e"]
    chunk = min(tl["chunk"], t)
    pp = min(tl["pp"], w // LANES)
    tn_f = min(tl["tn_f"], dff)

    n_mem_rows = nb * n_mem
    mem2d = mem_prompt.reshape(n_mem_rows, d)
    mk = _norm_matmul(mem2d, g_mem, w_k_b, tm=min(tl["tm_a"], n_mem_rows), tn=tn_a, out_dtype=F32, name="mem_k")
    mv = _norm_matmul(mem2d, g_mem, w_v_b, tm=min(tl["tm_a"], n_mem_rows), tn=tn_a, out_dtype=F32, name="mem_v")
    xp = x_prompt.reshape(nb * t, d)
    proj = _norm_matmul(xp, g_mix_pre, w_in_p, tm=tm_a, tn=tn_a, out_dtype=F32, name="proj_prompt")
    cv, conv_p = _conv_prompt(proj, nb, t, c, *conv_params, tt=min(tl["tt_conv"], t))
    tt_p = min(tl["tt_prep"], t)
    streams = _prep(proj, None, mu_p, *prep_params, n_seq_tiles=nb, tiles_per_seq=t // tt_p, tt=tt_p, c=c, w=w, lp=lp, sh=1, hn=hn)
    rw, wkv_p = _wkv(streams, *wkv_params, None, nb=nb, t=t, rows=chunk, w=w, hn=hn, pp=pp, out_dtype=BF16)
    x1 = _mm_norm_res([cv, rw], [w_out_b[:c], w_out_b[c:]], g_mix_post, xp, tm=tm_e, tk=tk_e, name="mix_out_prompt")
    q = _norm_matmul(x1, g_xa_pre, w_q_b, tm=tm_a, tn=tn_a, out_dtype=BF16, name="q_prompt")
    o = _attn_prompt(q, mk.reshape(nb, n_mem, xd), mv.reshape(nb, n_mem, xd), nb, t, xa_heads, tq=min(tl["tq"], t))
    x2 = _mm_norm_res([o], [w_o_b], g_xa_post, x1, tm=tm_e, tk=tk_e, name="attn_out_prompt")
    yp, ffn_p = _ffn_prompt(x2, g_ffn_pre, w_up_b, cw3, cb2, w_down_b, g_ffn_post, nb=nb, t=t, tm=min(tl["tm_f"], t), tn=tn_f)
    shift_p = unpad_rcols(proj.reshape(nb, t, -1)[:, -1, 2 * c:])

    rows_s = t_s * db
    xs_bm = x_sample.reshape(db, t_s * d)
    proj_s = _norm_matmul(xs_bm, g_mix_pre, w_in_p, tm=rows_s, tn=tn_a, out_dtype=F32, name="proj_sample")
    cv_s, conv_s = _conv_sample(proj_s, state_conv[l].reshape(db, (kc - 1) * c), db, t_s, c, *conv_params)
    streams_s = _prep(proj_s, pad_rcols(state_shift[l]), mu_p, *prep_params, n_seq_tiles=1, tiles_per_seq=t_s, tt=db,
                      c=c, w=w, lp=lp, sh=db, hn=hn, out_batch_major=True)
    streams_s = [x.reshape(rows_s, w) for x in streams_s]
    rw_s, wkv_s = _wkv(streams_s, *wkv_params, state_wkv[l].reshape(db, w // LANES, LANES, hn), nb=db, t=t_s,
                       rows=tl["chunk"], w=w, hn=hn, pp=pp, out_dtype=F32)
    x1_s = _mm_norm_res([cv_s, rw_s.reshape(db, t_s * w)], [w_out_b[:c], w_out_b[c:]], g_mix_post, xs_bm, tm=db, tk=tk_e,
                        name="mix_out_sample", a_batch_major=(False, True), res_batch_major=True)
    q_s = _norm_matmul(x1_s, g_xa_pre, w_q_b, tm=db, tn=tn_a, out_dtype=F32, name="q_sample", out_batch_major=True)
    o_s = _attn_sample(q_s.reshape(rows_s, xd), cache_mem_k, cache_mem_v, db, t_s)
    x2_s = _mm_norm_res([o_s.reshape(db, t_s * xd)], [w_o_b], g_xa_post, x1_s, tm=db, tk=tk_e, name="attn_out_sample",
                        a_batch_major=(True,))
    ys2, ffn_s = _ffn_sample(x2_s, state_ffn[l].reshape(db, 4 * dff), g_ffn_pre, w_up_b, cw3, cb2, w_down_b, g_ffn_post,
                             db=db, t_s=t_s, tn=tn_f)
    shift_s = unpad_rcols(proj_s[(t_s - 1) * db:, 2 * c:])

    return (yp.reshape(nb, t, d), ys2.reshape(db, t_s, d),
            conv_p[None], conv_s.reshape(db, kc - 1, c)[None],
            shift_p[None], shift_s[None],
            wkv_p.reshape(nb, n_heads, hn, hn)[None], wkv_s.reshape(db, n_heads, hn, hn)[None],
            ffn_p[None], ffn_s[None],
            mk.reshape(nb, n_mem, xa_heads, xa_hd)[None], mv.reshape(nb, n_mem, xa_heads, xa_hd)[None])
```

```python
import functools
import math

import jax
import jax.numpy as jnp
from jax import lax
from jax.experimental import pallas as pl
from jax.experimental.pallas import tpu as pltpu

F32 = jnp.float32
BF16 = jnp.bfloat16
RMS_EPS = 1e-6
LN_EPS = 1e-5
GN_EPS = 64e-5
LANES = 128
ROW_CHUNK = 16
NORM_UNROLL = 4
FFN_SUB = 256
DECAY_SCALE = math.exp(-0.5)
_NT = (((1,), (1,)), ((), ()))


_TILES = dict(
    tm_a=1024, tn_a=512,
    tm_e=512, tk_e=2048,
    tt_conv=256, tt_prep=256,
    chunk=64, pp=8,
    tq=256,
    tm_f=512, tn_f=512,
)


def _cparams(n_grid, vmem_mib=48):
    return pltpu.CompilerParams(dimension_semantics=("arbitrary",) * n_grid,
                                vmem_limit_bytes=vmem_mib * 1024 * 1024)


def _sigmoid(x):
    return 1.0 / (1.0 + jnp.exp(-x))


def _rms(x, g):
    return x * lax.rsqrt(jnp.mean(x * x, axis=-1, keepdims=True) + RMS_EPS) * g


def _row_loop(n_rows, rc, fn, unroll=1):
    def body(i, carry):
        fn(pl.multiple_of(i * rc, rc))
        return carry
    lax.fori_loop(0, n_rows // rc, body, 0, unroll=unroll)


def _res_norm_rows(n_rows, g, load_y, load_res, store):
    n_grp = math.gcd(NORM_UNROLL, n_rows // ROW_CHUNK)

    def group(r0):
        rs = [pl.multiple_of(r0 + k * ROW_CHUNK, ROW_CHUNK) for k in range(n_grp)]
        ys = [load_y(r) for r in rs]
        res = [load_res(r) for r in rs]
        outs = [x + _rms(y, g) for x, y in zip(res, ys)]
        for r, o in zip(rs, outs):
            store(r, o)
    _row_loop(n_rows, ROW_CHUNK * n_grp, group)


def _mm(x, y, precision=None):
    return jnp.dot(x, y, precision=precision, preferred_element_type=F32)


def _norm_rows_to(h_scr, x_ref, g_ref):
    g = g_ref[...]
    rows = x_ref.shape[0]
    d = h_scr.shape[1]
    for t in range(x_ref.shape[1] // d):
        def chunk(r0, t=t):
            dst = pl.multiple_of(t * rows + r0, ROW_CHUNK)
            h_scr[pl.ds(dst, ROW_CHUNK), :] = _rms(x_ref[pl.ds(r0, ROW_CHUNK), t * d:(t + 1) * d], g).astype(BF16)
        _row_loop(rows, ROW_CHUNK, chunk, unroll=NORM_UNROLL)


def _norm_matmul_body(x_ref, g_ref, w_ref, o_ref, h_scr):
    @pl.when(pl.program_id(1) == 0)
    def _():
        _norm_rows_to(h_scr, x_ref, g_ref)

    o_ref[...] = _mm(h_scr[...], w_ref[...]).astype(o_ref.dtype)


def _norm_matmul_pair_body(x_ref, g_ref, w1_ref, w2_ref, o1_ref, o2_ref, h_scr):
    @pl.when(pl.program_id(1) == 0)
    def _():
        _norm_rows_to(h_scr, x_ref, g_ref)

    o1_ref[...] = _mm(h_scr[...], w1_ref[...])
    o2_ref[...] = _mm(h_scr[...], w2_ref[...])


def _norm_matmul_pair(x, g, w1, w2, *, tm, tn, name):
    d, n = w1.shape
    m = x.shape[0]
    tn = min(tn, n)
    wspec = pl.BlockSpec((d, tn), lambda i, j: (0, j))
    ospec = pl.BlockSpec((tm, tn), lambda i, j: (i, j))
    return pl.pallas_call(
        _norm_matmul_pair_body,
        grid=(m // tm, n // tn),
        in_specs=[pl.BlockSpec((tm, d), lambda i, j: (i, 0)), pl.BlockSpec((1, d), lambda i, j: (0, 0)), wspec, wspec],
        out_specs=[ospec, ospec],
        out_shape=[jax.ShapeDtypeStruct((m, n), F32)] * 2,
        scratch_shapes=[pltpu.VMEM((tm, d), BF16)],
        compiler_params=_cparams(2),
        name=name,
    )(x, g, w1, w2)


def _norm_matmul(x, g, w, *, tm, tn, out_dtype, name, out_batch_major=False):
    d, n = w.shape
    n_t = x.shape[1] // d
    m = x.shape[0] * n_t
    assert n_t == 1 or m == tm
    tn = min(tn, n)
    nj = n // tn
    if out_batch_major:
        out_spec = pl.BlockSpec((tm, tn), lambda i, j: (0, i * nj + j))
        out_shape = jax.ShapeDtypeStruct((tm, (m // tm) * n), out_dtype)
    else:
        out_spec = pl.BlockSpec((tm, tn), lambda i, j: (i, j))
        out_shape = jax.ShapeDtypeStruct((m, n), out_dtype)
    return pl.pallas_call(
        _norm_matmul_body,
        grid=(m // tm, nj),
        in_specs=[pl.BlockSpec((tm // n_t, n_t * d), lambda i, j: (i, 0)),
                  pl.BlockSpec((1, d), lambda i, j: (0, 0)), pl.BlockSpec((d, tn), lambda i, j: (0, j))],
        out_specs=out_spec,
        out_shape=out_shape,
        scratch_shapes=[pltpu.VMEM((tm, d), BF16)],
        compiler_params=_cparams(2),
        name=name,
    )(x, g, w)


def _mm_norm_res_body(*refs, n_a):
    a_refs = refs[:n_a]
    w_refs = refs[n_a:2 * n_a]
    g_ref, r_ref, o_ref = refs[2 * n_a:2 * n_a + 3]
    k = pl.program_id(1)

    part = None
    for a, w in zip(a_refs, w_refs):
        d = _mm(a[...].astype(BF16), w[...])
        part = d if part is None else part + d

    @pl.when(k == 0)
    def _():
        o_ref[...] = part

    @pl.when(k > 0)
    def _():
        o_ref[...] += part

    @pl.when(k == pl.num_programs(1) - 1)
    def _():
        def store(r, v):
            o_ref[pl.ds(r, ROW_CHUNK), :] = v
        _res_norm_rows(o_ref.shape[0], g_ref[...], lambda r: o_ref[pl.ds(r, ROW_CHUNK), :],
                       lambda r: r_ref[pl.ds(r, ROW_CHUNK), :], store)


def _mm_norm_res(a_list, w_list, g, res, *, tm, tk, name, a_batch_major=(), res_batch_major=False):
    kdim, n = w_list[0].shape
    n_a = len(a_list)
    m = res.size // n
    tk = min(tk, kdim)
    nk = kdim // tk
    a_specs = []
    for idx in range(n_a):
        if idx < len(a_batch_major) and a_batch_major[idx]:
            a_specs.append(pl.BlockSpec((tm, tk), lambda i, k: (0, i * nk + k)))
        else:
            a_specs.append(pl.BlockSpec((tm, tk), lambda i, k: (i, k)))
    return pl.pallas_call(
        functools.partial(_mm_norm_res_body, n_a=n_a),
        grid=(m // tm, nk),
        in_specs=[*a_specs,
                  *[pl.BlockSpec((tk, n), lambda i, k: (k, 0)) for _ in w_list],
                  pl.BlockSpec((1, n), lambda i, k: (0, 0)),
                  pl.BlockSpec((tm, n), (lambda i, k: (0, i)) if res_batch_major else (lambda i, k: (i, 0)))],
        out_specs=pl.BlockSpec((tm, n), lambda i, k: (i, 0)),
        out_shape=jax.ShapeDtypeStruct((m, n), F32),
        compiler_params=_cparams(2),
        name=name,
    )(*a_list, *w_list, g, res)


def _ln_swish(y, lg, lb):
    mu = jnp.mean(y, axis=-1, keepdims=True)
    d = y - mu
    var = jnp.mean(d * d, axis=-1, keepdims=True)
    yn = d * lax.rsqrt(var + LN_EPS) * lg + lb
    return yn * _sigmoid(yn)


def _conv_prompt_body(a_ref, b_ref, w_ref, cb_ref, lg_ref, lb_ref, cv_ref, nc_ref, full, cvt, *, kw, lane_blk):
    tt, c = a_ref.shape
    past = kw - 1
    hp = -(-past // 8) * 8
    lead = hp - past
    ti = pl.program_id(1)

    @pl.when(ti == 0)
    def _():
        full[0:hp, :] = jnp.zeros((hp, c), F32)

    @pl.when(ti > 0)
    def _():
        full[0:hp, :] = full[tt:tt + hp, :]

    def glu(r0):
        dst = pl.multiple_of(hp + r0, ROW_CHUNK)
        full[pl.ds(dst, ROW_CHUNK), :] = a_ref[pl.ds(r0, ROW_CHUNK), :] * _sigmoid(b_ref[pl.ds(r0, ROW_CHUNK), :])
    _row_loop(tt, ROW_CHUNK, glu)

    rt = 32

    def taps(r0):
        for lb in range(c // lane_blk):
            ls = slice(lb * lane_blk, (lb + 1) * lane_blk)
            win = full[pl.ds(r0, rt + hp), ls]
            acc = jnp.broadcast_to(cb_ref[:, ls], (rt, lane_blk))
            for s in range(8):
                offs = [o for o in range(lead, lead + kw) if o % 8 == s]
                if not offs:
                    continue
                span = offs[-1] - s + rt
                shifted = win if s == 0 else win[s:s + span, :]
                for o in offs:
                    acc = acc + shifted[o - s:o - s + rt, :] * w_ref[o - lead:o - lead + 1, ls]
            cvt[pl.ds(r0, rt), ls] = acc
    _row_loop(tt, rt, taps)

    def ln(r0):
        cv_ref[pl.ds(r0, ROW_CHUNK), :] = _ln_swish(cvt[pl.ds(r0, ROW_CHUNK), :], lg_ref[...], lb_ref[...]).astype(cv_ref.dtype)
    _row_loop(tt, ROW_CHUNK, ln, unroll=2)

    @pl.when(ti == pl.num_programs(1) - 1)
    def _():
        nc_ref[...] = full[tt + lead:tt + hp, :]


def _conv_prompt(proj, nb, t, c, conv_w, conv_b, ln_g, ln_b, tt):
    kw = conv_w.shape[0]
    nt = t // tt
    hp = -(-(kw - 1) // 8) * 8
    vec = pl.BlockSpec((1, c), lambda b, i: (0, 0))
    return pl.pallas_call(
        functools.partial(_conv_prompt_body, kw=kw, lane_blk=LANES),
        grid=(nb, nt),
        in_specs=[pl.BlockSpec((tt, c), lambda b, i: (b * nt + i, 0)),
                  pl.BlockSpec((tt, c), lambda b, i: (b * nt + i, 1)),
                  pl.BlockSpec((kw, c), lambda b, i: (0, 0)), vec, vec, vec],
        out_specs=[pl.BlockSpec((tt, c), lambda b, i: (b * nt + i, 0)),
                   pl.BlockSpec((None, kw - 1, c), lambda b, i: (b, 0, 0))],
        out_shape=[jax.ShapeDtypeStruct((nb * t, c), BF16), jax.ShapeDtypeStruct((nb, kw - 1, c), F32)],
        scratch_shapes=[pltpu.VMEM((tt + hp, c), F32), pltpu.VMEM((tt, c), F32)],
        compiler_params=_cparams(2),
        name="conv_prompt",
    )(proj, proj, conv_w, conv_b, ln_g, ln_b)


def _conv_sample_body(a_ref, b_ref, st_ref, w_ref, cb_ref, lg_ref, lb_ref, cv_ref, nc_ref, glu, cvt, *, t_s, kw):
    db = st_ref.shape[0]
    c = w_ref.shape[1]
    past = kw - 1
    rows = t_s * db

    def make_glu(r0):
        glu[pl.ds(r0, ROW_CHUNK), :] = a_ref[pl.ds(r0, ROW_CHUNK), :] * _sigmoid(b_ref[pl.ds(r0, ROW_CHUNK), :])
    _row_loop(rows, ROW_CHUNK, make_glu)

    def taps(r0):
        for lb in range(c // LANES):
            ls = slice(lb * LANES, (lb + 1) * LANES)
            srcs = []
            for f in range(past + t_s):
                if f < past:
                    srcs.append(st_ref[pl.ds(r0, 8), f * c + ls.start:f * c + ls.stop])
                else:
                    srcs.append(glu[pl.ds(pl.multiple_of((f - past) * db + r0, 8), 8), ls])
            for t in range(t_s):
                acc = jnp.broadcast_to(cb_ref[:, ls], (8, LANES))
                for j in range(kw):
                    acc = acc + srcs[t + j] * w_ref[j:j + 1, ls]
                cvt[pl.ds(pl.multiple_of(t * db + r0, 8), 8), ls] = acc
    _row_loop(db, 8, taps)

    def ln(r0):
        cv_ref[pl.ds(r0, ROW_CHUNK), :] = _ln_swish(cvt[pl.ds(r0, ROW_CHUNK), :], lg_ref[...], lb_ref[...]).astype(cv_ref.dtype)
    _row_loop(rows, ROW_CHUNK, ln)

    for r in range(past):
        f = t_s + r
        if f < past:
            nc_ref[:, r * c:(r + 1) * c] = st_ref[:, f * c:(f + 1) * c]
        else:
            nc_ref[:, r * c:(r + 1) * c] = glu[(f - past) * db:(f - past + 1) * db, :]


def _conv_sample(proj, state2d, db, t_s, c, conv_w, conv_b, ln_g, ln_b):
    kw = conv_w.shape[0]
    rows = t_s * db
    vec = pl.BlockSpec((1, c), lambda i: (0, 0))
    return pl.pallas_call(
        functools.partial(_conv_sample_body, t_s=t_s, kw=kw),
        grid=(1,),
        in_specs=[pl.BlockSpec((rows, c), lambda i: (0, 0)),
                  pl.BlockSpec((rows, c), lambda i: (0, 1)),
                  pl.BlockSpec((db, (kw - 1) * c), lambda i: (0, 0)),
                  pl.BlockSpec((kw, c), lambda i: (0, 0)), vec, vec, vec],
        out_specs=[pl.BlockSpec((rows, c), lambda i: (0, 0)),
                   pl.BlockSpec((db, (kw - 1) * c), lambda i: (0, 0))],
        out_shape=[jax.ShapeDtypeStruct((rows, c), BF16), jax.ShapeDtypeStruct((db, (kw - 1) * c), F32)],
        scratch_shapes=[pltpu.VMEM((rows, c), F32), pltpu.VMEM((rows, c), F32)],
        compiler_params=_cparams(1, 56),
        name="conv_sample",
    )(proj, proj, state2d, conv_w, conv_b, ln_g, ln_b)


def _head_ones(hn):
    ri = lax.broadcasted_iota(jnp.int32, (LANES, LANES), 0)
    ci = lax.broadcasted_iota(jnp.int32, (LANES, LANES), 1)
    return (ri // hn) == (ci // hn)


def _prep_body(*refs, sh, has_state, hn, dlp, alp):
    pr = refs[0:4]
    i0 = 4
    st = None
    if has_state:
        st = refs[i0:i0 + 4]
        i0 += 4
    mu = refs[i0:i0 + 4]
    w0_ref, a0_ref, kk_ref, ka_ref, wl_ref, al_ref, gl_ref = refs[i0 + 4:i0 + 11]
    r_ref, lw_ref, k2_ref, v_ref, a_ref, b_ref, g_ref = refs[i0 + 11:i0 + 18]
    bufs = refs[i0 + 18:i0 + 22]
    lact = refs[i0 + 22]
    tt, w = r_ref.shape
    off = -(-sh // 8) * 8
    ti = pl.program_id(1)

    @pl.when(ti == 0)
    def _():
        for n, bf in enumerate(bufs):
            bf[0:off, :] = st[n][...] if has_state else jnp.zeros((off, bf.shape[1]), F32)

    @pl.when(ti > 0)
    def _():
        for bf in bufs:
            bf[0:off, :] = bf[tt:tt + off, :]

    def copy(r0):
        dst = pl.multiple_of(off + r0, 8)
        for bf, p in zip(bufs, pr):
            bf[pl.ds(dst, 8), :] = p[pl.ds(r0, 8), :]
    _row_loop(tt, 8, copy)

    def shifted(bf, m, r0):
        cur = bf[pl.ds(pl.multiple_of(off + r0, 8), 8), :]
        if sh % 8 == 0:
            prev = bf[pl.ds(pl.multiple_of(off - sh + r0, 8), 8), :]
        else:
            win = bf[pl.ds(r0, 16), :]
            prev = win[8 - sh:16 - sh, :]
        return cur + (prev - cur) * m[...]

    def stage1(r0):
        r_ref[pl.ds(r0, 8), :] = shifted(bufs[0], mu[0], r0)
        k = shifted(bufs[1], mu[1], r0)
        k2_ref[pl.ds(r0, 8), :] = k
        kk = k * kk_ref[...]
        b_ref[pl.ds(r0, 8), :] = kk * kk
        v_ref[pl.ds(r0, 8), :] = shifted(bufs[2], mu[2], r0)
    _row_loop(tt, 8, stage1)

    def stage1l(r0):
        xl = None
        for h in range(2):
            part = shifted(bufs[3], mu[3], r0 + 8 * h)
            xl = part if xl is None else jnp.concatenate([xl, part], axis=0)
        lane = lax.broadcasted_iota(jnp.int32, xl.shape, 1)
        act = jnp.where(lane < dlp, jnp.tanh(xl), jnp.where(lane < dlp + alp, xl, _sigmoid(xl)))
        lact[pl.ds(r0, ROW_CHUNK), :] = act.astype(BF16)
    _row_loop(tt, ROW_CHUNK, stage1l)

    lw_ref[...] = _mm(lact[:, 0:dlp], wl_ref[...])
    a_ref[...] = _mm(lact[:, dlp:dlp + alp], al_ref[...])
    g_ref[...] = _mm(lact[:, dlp + alp:], gl_ref[...])
    ones = jnp.where(_head_ones(hn), 1.0, 0.0).astype(BF16)
    for q in range(w // LANES):
        ls = slice(q * LANES, (q + 1) * LANES)
        b_ref[:, ls] = _dot_exact_rhs(_split3(b_ref[:, ls]), ones)

    def stage3(r0):
        rs = pl.ds(r0, 8)
        k = k2_ref[rs, :]
        lw_ref[rs, :] = -DECAY_SCALE * _sigmoid(w0_ref[...] + lw_ref[rs, :])
        asig = _sigmoid(a0_ref[...] + a_ref[rs, :])
        kk = k * kk_ref[...]
        kkn = kk / jnp.maximum(jnp.sqrt(b_ref[rs, :]), 1e-12)
        k2_ref[rs, :] = k * (1.0 + (asig - 1.0) * ka_ref[...])
        a_ref[rs, :] = -kkn
        b_ref[rs, :] = kkn * asig
    _row_loop(tt, 8, stage3)


def _prep(proj, st, mu_p, w0, a0, k_k, k_a, wl, al, gl, *, n_seq_tiles, tiles_per_seq, tt, c, w, lp, sh, hn,
          out_batch_major=False):
    dlp, alp = wl.shape[0], al.shape[0]
    has_state = st is not None
    rows = n_seq_tiles * tiles_per_seq * tt
    cb = 2 * c // w
    lb = (2 * c + 3 * w) // lp
    off = -(-sh // 8) * 8
    rmap = lambda o: (lambda s, i: (s * tiles_per_seq + i, o))
    in_specs = [pl.BlockSpec((tt, w), rmap(cb)), pl.BlockSpec((tt, w), rmap(cb + 1)),
                pl.BlockSpec((tt, w), rmap(cb + 2)), pl.BlockSpec((tt, lp), rmap(lb))]
    args = [proj, proj, proj, proj]
    if has_state:
        in_specs += [pl.BlockSpec((sh, w), lambda s, i: (0, 0)), pl.BlockSpec((sh, w), lambda s, i: (0, 1)),
                     pl.BlockSpec((sh, w), lambda s, i: (0, 2)), pl.BlockSpec((sh, lp), lambda s, i: (0, 3 * w // lp))]
        args += [st, st, st, st]
    in_specs += [pl.BlockSpec((1, w), lambda s, i: (0, 0)), pl.BlockSpec((1, w), lambda s, i: (0, 1)),
                 pl.BlockSpec((1, w), lambda s, i: (0, 2)), pl.BlockSpec((1, lp), lambda s, i: (0, 3 * w // lp))]
    args += [mu_p, mu_p, mu_p, mu_p]
    vec = pl.BlockSpec((1, w), lambda s, i: (0, 0))
    in_specs += [vec, vec, vec, vec,
                 pl.BlockSpec(wl.shape, lambda s, i: (0, 0)), pl.BlockSpec(al.shape, lambda s, i: (0, 0)),
                 pl.BlockSpec(gl.shape, lambda s, i: (0, 0))]
    args += [w0, a0, k_k, k_a, wl, al, gl]
    if out_batch_major:
        assert n_seq_tiles == 1
        ospec = pl.BlockSpec((tt, w), lambda s, i: (0, i))
        oshape = jax.ShapeDtypeStruct((tt, tiles_per_seq * w), F32)
    else:
        ospec = pl.BlockSpec((tt, w), lambda s, i: (s * tiles_per_seq + i, 0))
        oshape = jax.ShapeDtypeStruct((rows, w), F32)
    return pl.pallas_call(
        functools.partial(_prep_body, sh=sh, has_state=has_state, hn=hn, dlp=dlp, alp=alp),
        grid=(n_seq_tiles, tiles_per_seq),
        in_specs=in_specs,
        out_specs=[ospec] * 7,
        out_shape=[oshape] * 7,
        scratch_shapes=[pltpu.VMEM((off + tt, w), F32)] * 3 + [pltpu.VMEM((off + tt, lp), F32), pltpu.VMEM((tt, lp), BF16)],
        compiler_params=_cparams(2, 56),
        name="rwkv_prep_state" if has_state else "rwkv_prep",
    )(*args)


def _split2(x):
    hi = x.astype(BF16)
    lo = (x - hi.astype(F32)).astype(BF16)
    return hi, lo


def _split3(x):
    hi = x.astype(BF16)
    r1 = x - hi.astype(F32)
    mid = r1.astype(BF16)
    lo = (r1 - mid.astype(F32)).astype(BF16)
    return hi, mid, lo


def _dot3(xs, ys, dims=(((1,), (0,)), ((), ()))):
    def d(a, b):
        return lax.dot_general(a, b, dims, preferred_element_type=F32)
    return d(xs[0], ys[0]) + (d(xs[0], ys[1]) + d(xs[1], ys[0]))


def _dot_exact_rhs(parts, y_bf16):
    out = None
    for p in parts[::-1]:
        d = _mm(p, y_bf16)
        out = d if out is None else out + d
    return out


def _wkv_body(*refs, sub, pp, has_state, hn):
    r_ref, lw_ref, k_ref, v_ref, a_ref, b_ref, g_ref, rk_ref, lg_ref, lb_ref = refs[:10]
    i0 = 10
    s0_ref = None
    if has_state:
        s0_ref = refs[i0]
        i0 += 1
    o_ref, so_ref, sbd = refs[i0:i0 + 3]
    ell = r_ref.shape[0]
    l2 = 2 * ell
    nseq = ell // sub
    n_iter = max(1, int(math.log2(sub)))
    c = pl.program_id(2)
    bd = _head_ones(hn)
    lane = lax.broadcasted_iota(jnp.int32, (1, LANES), 1)
    lo = lane < hn
    row_lo = lax.broadcasted_iota(jnp.int32, (LANES, LANES), 0) < hn

    @pl.when(c == 0)
    def _():
        if has_state:
            for s in range(nseq):
                for q in range(pp):
                    sbd[s * pp + q] = jnp.zeros((LANES, LANES), F32)
                    sbd[s * pp + q, :, 0:hn] = s0_ref[s, q]
                    x = sbd[s * pp + q]
                    sbd[s * pp + q] = jnp.where(bd, x + pltpu.roll(x, hn, 1), 0.0)
        else:
            sbd[...] = jnp.zeros(sbd.shape, F32)

    lw = lw_ref[...]
    ri = lax.broadcasted_iota(jnp.int32, (ell, ell), 0)
    ci = lax.broadcasted_iota(jnp.int32, (ell, ell), 1)
    tri = jnp.where((ri >= ci) & ((ri // sub) == (ci // sub)), 1.0, 0.0).astype(BF16)
    cs = None
    for part in _split3(lw)[::-1]:
        d = _mm(tri, part)
        cs = d if cs is None else cs + d
    w_in = jnp.exp(cs)
    w_inv = jnp.exp(-cs)
    a_t = a_ref[...] * jnp.exp(cs - lw)
    r_t = r_ref[...] * w_in
    b_t = b_ref[...] * w_inv
    k_t = k_ref[...] * w_inv
    rr = lax.broadcasted_iota(jnp.int32, (l2, l2), 0) % ell
    cc = lax.broadcasted_iota(jnp.int32, (l2, l2), 1) % ell
    same_seq = (rr // sub) == (cc // sub)
    strict = (cc < rr) & same_seq
    incl = (cc <= rr) & same_seq
    ones = jnp.where(bd, 1.0, 0.0).astype(BF16)
    avg = jnp.where(bd, 1.0 / hn, 0.0).astype(BF16)

    def seq_rows(x, s):
        return jnp.concatenate([x[s * sub:(s + 1) * sub], x[ell + s * sub:ell + (s + 1) * sub]], axis=0)

    def head_stack(pieces):
        return jnp.concatenate([p[0:sub] for p in pieces] + [p[sub:2 * sub] for p in pieces], axis=0)

    pairs = range(pp)
    lss = [slice(q * LANES, (q + 1) * LANES) for q in pairs]

    def blk(x, q):
        xs = x[:, lss[q]]
        return jnp.concatenate([jnp.where(lo, xs, 0.0), jnp.where(lo, 0.0, xs)], axis=0)

    ab = [blk(a_t, q) for q in pairs]
    rb = [blk(r_t, q) for q in pairs]
    bb = [blk(b_t, q) for q in pairs]
    kb = [blk(k_t, q) for q in pairs]
    vb = [blk(v_ref[...], q) for q in pairs]
    a_s = [_split2(ab[q]) for q in pairs]
    r_hi = [rb[q].astype(BF16) for q in pairs]
    bk_s = [_split2(jnp.concatenate([bb[q], kb[q]], axis=0)) for q in pairs]
    v_s = [_split2(vb[q]) for q in pairs]
    g_a = [_dot3(a_s[q], bk_s[q], _NT) for q in pairs]
    g_r = [lax.dot_general(r_hi[q], bk_s[q][0], _NT, preferred_element_type=F32) for q in pairs]
    m_s = [_split2(jnp.where(strict, g_a[q][:, 0:l2], 0.0)) for q in pairs]
    m_ak = [_split2(jnp.where(strict, g_a[q][:, l2:], 0.0)) for q in pairs]
    m_r = [jnp.concatenate([jnp.where(incl, g_r[q][:, 0:l2], 0.0), jnp.where(incl, g_r[q][:, l2:], 0.0)],
                           axis=1).astype(BF16) for q in pairs]

    ps_a, ps_r = [], []
    for q in pairs:
        if nseq == 1:
            s_s = _split2(sbd[q])
            ps_a.append(_dot3(a_s[q], s_s, _NT))
            ps_r.append(lax.dot_general(r_hi[q], s_s[0], _NT, preferred_element_type=F32))
        else:
            pa, pr = [], []
            for s in range(nseq):
                sel = jnp.concatenate([seq_rows(ab[q], s), seq_rows(rb[q], s)], axis=0)
                ps = _dot3(_split2(sel), _split2(sbd[s * pp + q]), _NT)
                pa.append(ps[0:2 * sub])
                pr.append(ps[2 * sub:])
            ps_a.append(head_stack(pa))
            ps_r.append(head_stack(pr))

    u = [ps_a[q] + _dot3(m_ak[q], v_s[q]) for q in pairs]
    for it in range(n_iter):
        u = [u[q] + _dot_exact_rhs(m_s[q], u[q].astype(BF16)) for q in pairs]
        if it < n_iter - 1:
            m_s = [_split2(_dot_exact_rhs(m_s[q], m_s[q][0])) for q in pairs]
    uv = [jnp.concatenate([u[q], vb[q]], axis=0) for q in pairs]
    yb = [ps_r[q] + _mm(m_r[q], uv[q].astype(BF16)) for q in pairs]
    y = [yb[q][0:ell, :] + yb[q][ell:l2, :] for q in pairs]

    for q in pairs:
        for s in range(nseq):
            if nseq == 1:
                uv_t, bk_sel = uv[q].T, bk_s[q]
            else:
                uv_t = jnp.concatenate([seq_rows(u[q], s), seq_rows(vb[q], s)], axis=0).T
                bk_sel = _split2(jnp.concatenate([seq_rows(bb[q], s), seq_rows(kb[q], s)], axis=0))
            upd = _dot3(_split2(uv_t), bk_sel)
            last = (s + 1) * sub - 1
            sbd[s * pp + q] = (sbd[s * pp + q] + upd) * w_in[last:last + 1, lss[q]]

    mu = [_dot_exact_rhs(_split2(y[q]), avg) for q in pairs]
    d = [y[q] - mu[q] for q in pairs]
    var = [_dot_exact_rhs(_split2(d[q] * d[q]), avg) for q in pairs]
    bonus = [_dot_exact_rhs(_split2(r_ref[:, lss[q]] * k_ref[:, lss[q]] * rk_ref[:, lss[q]]), ones) for q in pairs]
    for q in pairs:
        ls = lss[q]
        yn = d[q] * lax.rsqrt(var[q] + GN_EPS) * lg_ref[:, ls] + lb_ref[:, ls]
        o_ref[:, ls] = ((yn + bonus[q] * v_ref[:, ls]) * g_ref[:, ls]).astype(o_ref.dtype)

    @pl.when(c == pl.num_programs(2) - 1)
    def _():
        for s in range(nseq):
            for q in range(pp):
                x = sbd[s * pp + q]
                so_ref[s, q] = jnp.where(row_lo, x, pltpu.roll(x, hn, 1))[:, 0:hn]


def _wkv(streams, r_k, ln_g, ln_b, s0, *, nb, t, rows, w, hn, pp, out_dtype):
    pw = pp * LANES
    ng = w // pw
    has_state = s0 is not None
    sub = min(t, rows)
    nseq = rows // sub
    n_chunks = t // sub
    assert n_chunks == 1 or nseq == 1
    sspec = pl.BlockSpec((rows, pw), lambda b, g, c: (b * n_chunks + c, g))
    vec = pl.BlockSpec((1, pw), lambda b, g, c: (0, g))
    stspec = pl.BlockSpec((nseq, pp, LANES, hn), lambda b, g, c: (b, g, 0, 0))
    in_specs = [sspec] * 7 + [vec] * 3 + ([stspec] if has_state else [])
    args = list(streams) + [r_k, ln_g, ln_b] + ([s0] if has_state else [])
    out, s_new = pl.pallas_call(
        functools.partial(_wkv_body, sub=sub, pp=pp, has_state=has_state, hn=hn),
        grid=(nb // nseq, ng, n_chunks),
        in_specs=in_specs,
        out_specs=[sspec, stspec],
        out_shape=[jax.ShapeDtypeStruct((nb * t, w), out_dtype), jax.ShapeDtypeStruct((nb, w // LANES, LANES, hn), F32)],
        scratch_shapes=[pltpu.VMEM((nseq * pp, LANES, LANES), F32)],
        compiler_params=_cparams(3),
        name="wkv_state" if has_state else "wkv",
    )(*args)
    return out, s_new


def _attn_body(q_ref, k_ref, v_ref, o_ref, kb_ref, vb_ref, *, nh, scale):
    @pl.when(pl.program_id(1) == 0)
    def _():
        kb_ref[...] = k_ref[...].astype(BF16)
        vb_ref[...] = v_ref[...].astype(BF16)
    hd = q_ref.shape[1] // nh
    for h in range(nh):
        hs = slice(h * hd, (h + 1) * hd)
        kh, vh = kb_ref[:, hs], vb_ref[:, hs]
        s = lax.dot_general(q_ref[:, hs].astype(BF16), kh, _NT, preferred_element_type=F32) * scale
        p = jnp.exp(s - jnp.max(s, axis=-1, keepdims=True))
        p = p / jnp.sum(p, axis=-1, keepdims=True)
        o_ref[:, hs] = _mm(p.astype(BF16), vh).astype(o_ref.dtype)


def _attn_prompt(q, k, v, nb, t, nh, tq):
    nm, xd = k.shape[1], k.shape[2]
    nt = t // tq
    kv = pl.BlockSpec((None, nm, xd), lambda b, i: (b, 0, 0))
    return pl.pallas_call(
        functools.partial(_attn_body, nh=nh, scale=(xd // nh) ** -0.5),
        grid=(nb, nt),
        in_specs=[pl.BlockSpec((tq, xd), lambda b, i: (b * nt + i, 0)), kv, kv],
        out_specs=pl.BlockSpec((tq, xd), lambda b, i: (b * nt + i, 0)),
        out_shape=jax.ShapeDtypeStruct((nb * t, xd), BF16),
        scratch_shapes=[pltpu.VMEM((nm, xd), BF16)] * 2,
        compiler_params=_cparams(2),
        name="attn_prompt",
    )(q, k, v)


def _attn_sample_body(q_ref, k_ref, v_ref, o_ref, *, scale):
    nm, nh, hd = k_ref.shape
    t_s = q_ref.shape[0]
    k2 = k_ref[...].reshape(nm * nh, hd).astype(BF16)
    v2 = v_ref[...].reshape(nm * nh, hd).astype(BF16)
    q4 = jnp.concatenate([q_ref[:, h * hd:(h + 1) * hd] for h in range(nh)], axis=0).astype(BF16)
    s = lax.dot_general(q4, k2, _NT, preferred_element_type=F32) * scale
    row_head = lax.broadcasted_iota(jnp.int32, s.shape, 0) // t_s
    col_head = lax.broadcasted_iota(jnp.int32, s.shape, 1) % nh
    s = jnp.where(row_head == col_head, s, -1e30)
    p = jnp.exp(s - jnp.max(s, axis=-1, keepdims=True))
    p = p / jnp.sum(p, axis=-1, keepdims=True)
    o4 = _mm(p.astype(BF16), v2)
    for h in range(nh):
        o_ref[:, h * hd:(h + 1) * hd] = o4[h * t_s:(h + 1) * t_s, :]


def _attn_sample(q, k, v, db, t_s):
    _, _, nm, nh, hd = k.shape
    xd = nh * hd
    qs = pl.BlockSpec((t_s, xd), lambda b: (b, 0))
    kv = pl.BlockSpec((None, None, nm, nh, hd), lambda b: (0, b, 0, 0, 0))
    return pl.pallas_call(
        functools.partial(_attn_sample_body, scale=hd ** -0.5),
        grid=(db,),
        in_specs=[qs, kv, kv],
        out_specs=qs,
        out_shape=jax.ShapeDtypeStruct((db * t_s, xd), F32),
        compiler_params=_cparams(1),
        name="attn_sample",
    )(q, k, v)


def _ffn_body(*refs, sh, t_s, has_state, tiles_per_seq):
    x_ref, gpre_ref, wua_ref, wub_ref, cw_ref, cb_ref, wd_ref, gpost_ref = refs[:8]
    i0 = 8
    st = None
    if has_state:
        st = refs[i0:i0 + 4]
        i0 += 4
    o_ref = refs[i0]
    i0 += 1
    n_nf = 4 if has_state else 1
    nf_refs = refs[i0:i0 + n_nf]
    i0 += n_nf
    hn_scr = refs[i0]
    nsb = (len(refs) - i0 - 1 - (0 if has_state else 1)) // 3
    fas = refs[i0 + 1:i0 + 1 + nsb]
    fbs = refs[i0 + 1 + nsb:i0 + 1 + 2 * nsb]
    acts = refs[i0 + 1 + 2 * nsb:i0 + 1 + 3 * nsb]
    carry = None if has_state else refs[i0 + 1 + 3 * nsb]
    tm, d = x_ref.shape
    sb = fas[0].shape[1]
    tn = nsb * sb
    off = max(8, 2 * sh)
    n_t = o_ref.shape[1] // d
    rows = tm // n_t
    i = pl.program_id(0)
    j = pl.program_id(1)

    @pl.when(j == 0)
    def _():
        _norm_rows_to(hn_scr, x_ref, gpre_ref)
        o_ref[...] = jnp.zeros(o_ref.shape, F32)

    if has_state:
        for s in range(nsb):
            cs = slice(s * sb, (s + 1) * sb)
            fas[s][0:sh, :] = st[0][:, cs]
            fbs[s][0:sh, :] = st[1][:, cs]
            fas[s][sh:2 * sh, :] = st[2][:, cs]
            fbs[s][sh:2 * sh, :] = st[3][:, cs]
    else:
        first = (i % tiles_per_seq) == 0

        @pl.when(first)
        def _():
            for s in range(nsb):
                fas[s][0:off, :] = jnp.zeros((off, sb), F32)
                fbs[s][0:off, :] = jnp.zeros((off, sb), F32)

        @pl.when(jnp.logical_not(first))
        def _():
            for s in range(nsb):
                fas[s][0:off, :] = carry[j, :, s * sb:(s + 1) * sb]
                fbs[s][0:off, :] = carry[j, :, tn + s * sb:tn + (s + 1) * sb]

    def up(s):
        cs = slice(s * sb, (s + 1) * sb)
        fas[s][off:off + tm, :] = _mm(hn_scr[...], wua_ref[:, cs])
        fbs[s][off:off + tm, :] = _mm(hn_scr[...], wub_ref[:, cs])

    def conv_act(s):
        cs = slice(s * sb, (s + 1) * sb)

        def conv(f, h, r0):
            if sh % 8 == 0:
                s2 = f[off - 2 * sh + r0:off - 2 * sh + r0 + ROW_CHUNK, :]
                s1 = f[off - sh + r0:off - sh + r0 + ROW_CHUNK, :]
                s0 = f[off + r0:off + r0 + ROW_CHUNK, :]
            else:
                win = f[r0:r0 + ROW_CHUNK + 8, :]
                s2 = win[8 - 2 * sh:8 - 2 * sh + ROW_CHUNK, :]
                s1 = win[8 - sh:8 - sh + ROW_CHUNK, :]
                s0 = win[8:8 + ROW_CHUNK, :]
            return (cb_ref[h:h + 1, cs] + s2 * cw_ref[0, h:h + 1, cs] + s1 * cw_ref[1, h:h + 1, cs]
                    + s0 * cw_ref[2, h:h + 1, cs])
        for r0 in range(0, tm, ROW_CHUNK):
            ua = conv(fas[s], 0, r0)
            ub = conv(fbs[s], 1, r0)
            acts[s][r0:r0 + ROW_CHUNK, :] = (ua * _sigmoid(ua) * ub).astype(BF16)

    def down(s):
        part = _mm(acts[s][...], wd_ref[s * sb:(s + 1) * sb, :])
        for t in range(n_t):
            o_ref[:, t * d:(t + 1) * d] += part[t * rows:(t + 1) * rows, :]

    up(0)
    for s in range(1, nsb):
        up(s)
        conv_act(s - 1)
        down(s - 1)
    conv_act(nsb - 1)
    down(nsb - 1)

    if not has_state:
        for s in range(nsb):
            carry[j, :, s * sb:(s + 1) * sb] = fas[s][tm:tm + off, :]
            carry[j, :, tn + s * sb:tn + (s + 1) * sb] = fbs[s][tm:tm + off, :]

    for s in range(nsb):
        cs = slice(s * sb, (s + 1) * sb)
        if has_state:
            for r in range(2):
                src = off + (t_s - 2 + r) * sh
                nf_refs[2 * r][:, cs] = fas[s][src:src + sh, :]
                nf_refs[2 * r + 1][:, cs] = fbs[s][src:src + sh, :]
        else:
            nf = nf_refs[0]
            for r in range(2):
                src = off + tm - 2 + r
                nf[r, 0:1, cs] = fas[s][src:src + 1, :]
                nf[r, 1:2, cs] = fbs[s][src:src + 1, :]

    @pl.when(j == pl.num_programs(1) - 1)
    def _():
        for t in range(n_t):
            ts = slice(t * d, (t + 1) * d)

            def store(r, v, ts=ts):
                o_ref[pl.ds(r, ROW_CHUNK), ts] = v
            _res_norm_rows(rows, gpost_ref[...], lambda r, ts=ts: o_ref[pl.ds(r, ROW_CHUNK), ts],
                           lambda r, t=t: x_ref[pl.ds(pl.multiple_of(t * rows + r, ROW_CHUNK), ROW_CHUNK), :], store)


def _ffn_prompt(x, g_pre, w_up, cw3, cb2, w_down, g_post, *, nb, t, tm, tn):
    d = x.shape[1]
    dff = w_down.shape[0]
    nj = dff // tn
    sb = min(FFN_SUB, tn)
    nsb = tn // sb
    tps = t // tm
    vec = pl.BlockSpec((1, d), lambda i, j: (0, 0))
    out, nf = pl.pallas_call(
        functools.partial(_ffn_body, sh=1, t_s=None, has_state=False, tiles_per_seq=tps),
        grid=(nb * tps, nj),
        in_specs=[pl.BlockSpec((tm, d), lambda i, j: (i, 0)), vec,
                  pl.BlockSpec((d, tn), lambda i, j: (0, j)), pl.BlockSpec((d, tn), lambda i, j: (0, j + nj)),
                  pl.BlockSpec((3, 2, tn), lambda i, j: (0, 0, j)), pl.BlockSpec((2, tn), lambda i, j: (0, j)),
                  pl.BlockSpec((tn, d), lambda i, j: (j, 0)), vec],
        out_specs=[pl.BlockSpec((tm, d), lambda i, j: (i, 0)),
                   pl.BlockSpec((None, 2, 2, tn), lambda i, j: (i, 0, 0, j))],
        out_shape=[jax.ShapeDtypeStruct((nb * t, d), F32), jax.ShapeDtypeStruct((nb * tps, 2, 2, dff), F32)],
        scratch_shapes=[pltpu.VMEM((tm, d), BF16),
                        *[pltpu.VMEM((8 + tm, sb), F32)] * (2 * nsb), *[pltpu.VMEM((tm, sb), BF16)] * nsb,
                        pltpu.VMEM((nj, 8, 2 * tn), F32)],
        compiler_params=_cparams(2, 56),
        name="ffn_prompt",
    )(x, g_pre, w_up, w_up, cw3, cb2, w_down, g_post)
    return out, nf.reshape(nb, tps, 2, 2 * dff)[:, -1]


def _ffn_sample(x_tm, st2d, g_pre, w_up, cw3, cb2, w_down, g_post, *, db, t_s, tn):
    d = x_tm.shape[1]
    dff = w_down.shape[0]
    nj = dff // tn
    sb = min(FFN_SUB, tn)
    nsb = tn // sb
    tm = t_s * db
    vec = pl.BlockSpec((1, d), lambda i, j: (0, 0))
    st_specs = [pl.BlockSpec((db, tn), (lambda i, j, o=o: (0, o * nj + j))) for o in range(4)]
    nf_spec = pl.BlockSpec((db, tn), lambda i, j: (0, j))
    out, n0a, n0b, n1a, n1b = pl.pallas_call(
        functools.partial(_ffn_body, sh=db, t_s=t_s, has_state=True, tiles_per_seq=1),
        grid=(1, nj),
        in_specs=[pl.BlockSpec((tm, d), lambda i, j: (0, 0)), vec,
                  pl.BlockSpec((d, tn), lambda i, j: (0, j)), pl.BlockSpec((d, tn), lambda i, j: (0, j + nj)),
                  pl.BlockSpec((3, 2, tn), lambda i, j: (0, 0, j)), pl.BlockSpec((2, tn), lambda i, j: (0, j)),
                  pl.BlockSpec((tn, d), lambda i, j: (j, 0)), vec, *st_specs],
        out_specs=[pl.BlockSpec((db, t_s * d), lambda i, j: (0, 0)), nf_spec, nf_spec, nf_spec, nf_spec],
        out_shape=[jax.ShapeDtypeStruct((db, t_s * d), F32)] + [jax.ShapeDtypeStruct((db, dff), F32)] * 4,
        scratch_shapes=[pltpu.VMEM((tm, d), BF16),
                        *[pltpu.VMEM((2 * db + tm, sb), F32)] * (2 * nsb), *[pltpu.VMEM((tm, sb), BF16)] * nsb],
        compiler_params=_cparams(2, 56),
        name="ffn_sample",
    )(x_tm, g_pre, w_up, w_up, cw3, cb2, w_down, g_post, st2d, st2d, st2d, st2d)
    new_ffn = jnp.stack([jnp.concatenate([n0a, n0b], axis=1), jnp.concatenate([n1a, n1b], axis=1)], axis=1)
    return out, new_ffn


def _pad_cols(x, width):
    return jnp.pad(x, [(0, 0)] * (x.ndim - 1) + [(0, width - x.shape[-1])])


def _round_up(n, m):
    return -(-n // m) * m


def kernel(x_prompt, x_sample, cache_mem_k, cache_mem_v, state_conv, state_shift, state_wkv, state_ffn, mem_prompt, norm_mix_pre, w_in, conv_dw, conv_dw_b, conv_ln_g, conv_ln_b, rwkv_mu, w0, w_lora, a0, a_lora, g_lora, k_k, k_a, r_k, ln_x_g, ln_x_b, w_out, norm_mix_post, norm_xa_pre, norm_mem, w_q, w_k, w_v, w_o, norm_xa_post, norm_ffn_pre, w_up, ffn_dw, ffn_dw_b, w_down, norm_ffn_post):
    nb, t, d = x_prompt.shape
    db, t_s, _ = x_sample.shape
    depth = w_in.shape[0]
    c = conv_dw.shape[-1]
    w = w0.shape[-1]
    n_heads, hn = state_wkv.shape[2], state_wkv.shape[3]
    dl, al, gl = w_lora.shape[1], a_lora.shape[1], g_lora.shape[1]
    dlp, alp, glp = _round_up(dl, LANES), _round_up(al, LANES), _round_up(gl, LANES)
    lp = dlp + alp + glp
    n_mem, xa_heads, xa_hd = cache_mem_k.shape[2:]
    xd = xa_heads * xa_hd
    dff = w_down.shape[1]
    kc = conv_dw.shape[1]
    assert depth == 1 and c == w and (2 * c + 3 * w) % lp == 0 and (3 * w) % lp == 0 and hn * 2 == LANES

    def pad_rcols(x):
        o = 3 * w
        return jnp.concatenate([x[..., :o], _pad_cols(x[..., o:o + dl], dlp), _pad_cols(x[..., o + dl:o + dl + al], alp),
                                _pad_cols(x[..., o + dl + al:], glp)], axis=-1)

    def unpad_rcols(x):
        o = 3 * w
        return jnp.concatenate([x[..., :o + dl], x[..., o + dlp:o + dlp + al], x[..., o + dlp + alp:o + dlp + alp + gl]], axis=-1)

    row = lambda v: v.reshape(1, -1)
    l = 0
    w_in_p = jnp.concatenate([w_in[l][:, :2 * c], pad_rcols(w_in[l][:, 2 * c:])], axis=1).astype(BF16)
    mu_p = pad_rcols(rwkv_mu[l]).reshape(1, -1)
    wl_p = jnp.pad(w_lora[l], ((0, dlp - dl), (0, 0))).astype(BF16)
    al_p = jnp.pad(a_lora[l], ((0, alp - al), (0, 0))).astype(BF16)
    gl_p = jnp.pad(g_lora[l], ((0, glp - gl), (0, 0))).astype(BF16)
    w_out_b, w_q_b, w_k_b, w_v_b, w_o_b = (x[l].astype(BF16) for x in (w_out, w_q, w_k, w_v, w_o))
    w_up_b, w_down_b = w_up[l].astype(BF16), w_down[l].astype(BF16)
    cw3 = ffn_dw[l].reshape(ffn_dw.shape[1], 2, dff)
    cb2 = ffn_dw_b[l].reshape(2, dff)
    g_mix_pre, g_mix_post, g_xa_pre, g_mem, g_xa_post, g_ffn_pre, g_ffn_post = (
        row(x[l]) for x in (norm_mix_pre, norm_mix_post, norm_xa_pre, norm_mem, norm_xa_post, norm_ffn_pre, norm_ffn_post))
    prep_params = (row(w0[l]), row(a0[l]), row(k_k[l]), row(k_a[l]), wl_p, al_p, gl_p)
    wkv_params = (row(r_k[l]), row(ln_x_g[l]), row(ln_x_b[l]))
    conv_params = (conv_dw[l], row(conv_dw_b[l]), row(conv_ln_g[l]), row(conv_ln_b[l]))

    tl = _TILES
    tm_a = min(tl["tm_a"], nb * t)
    tn_a = tl["tn_a"]
    tm_e = min(tl["tm_e"], nb * t)
    tk_e = tl["tk_e"]
    chunk = min(tl["chunk"], t)
    pp = min(tl["pp"], w // LANES)
    tn_f = min(tl["tn_f"], dff)

    n_mem_rows = nb * n_mem
    mem2d = mem_prompt.reshape(n_mem_rows, d)
    mk, mv = _norm_matmul_pair(mem2d, g_mem, w_k_b, w_v_b, tm=min(tl["tm_a"], n_mem_rows), tn=tn_a, name="mem_kv")
    xp = x_prompt.reshape(nb * t, d)
    proj = _norm_matmul(xp, g_mix_pre, w_in_p, tm=tm_a, tn=tn_a, out_dtype=F32, name="proj_prompt")
    cv, conv_p = _conv_prompt(proj, nb, t, c, *conv_params, tt=min(tl["tt_conv"], t))
    tt_p = min(tl["tt_prep"], t)
    streams = _prep(proj, None, mu_p, *prep_params, n_seq_tiles=nb, tiles_per_seq=t // tt_p, tt=tt_p, c=c, w=w, lp=lp, sh=1, hn=hn)
    rw, wkv_p = _wkv(streams, *wkv_params, None, nb=nb, t=t, rows=chunk, w=w, hn=hn, pp=pp, out_dtype=BF16)
    x1 = _mm_norm_res([cv, rw], [w_out_b[:c], w_out_b[c:]], g_mix_post, xp, tm=tm_e, tk=tk_e, name="mix_out_prompt")
    q = _norm_matmul(x1, g_xa_pre, w_q_b, tm=tm_a, tn=tn_a, out_dtype=BF16, name="q_prompt")
    o = _attn_prompt(q, mk.reshape(nb, n_mem, xd), mv.reshape(nb, n_mem, xd), nb, t, xa_heads, tq=min(tl["tq"], t))
    x2 = _mm_norm_res([o], [w_o_b], g_xa_post, x1, tm=tm_e, tk=tk_e, name="attn_out_prompt")
    yp, ffn_p = _ffn_prompt(x2, g_ffn_pre, w_up_b, cw3, cb2, w_down_b, g_ffn_post, nb=nb, t=t, tm=min(tl["tm_f"], t), tn=tn_f)
    shift_p = unpad_rcols(proj.reshape(nb, t, -1)[:, -1, 2 * c:])

    rows_s = t_s * db
    xs_bm = x_sample.reshape(db, t_s * d)
    proj_s = _norm_matmul(xs_bm, g_mix_pre, w_in_p, tm=rows_s, tn=tn_a, out_dtype=F32, name="proj_sample")
    cv_s, conv_s = _conv_sample(proj_s, state_conv[l].reshape(db, (kc - 1) * c), db, t_s, c, *conv_params)
    streams_s = _prep(proj_s, pad_rcols(state_shift[l]), mu_p, *prep_params, n_seq_tiles=1, tiles_per_seq=t_s, tt=db,
                      c=c, w=w, lp=lp, sh=db, hn=hn, out_batch_major=True)
    streams_s = [x.reshape(rows_s, w) for x in streams_s]
    rw_s, wkv_s = _wkv(streams_s, *wkv_params, state_wkv[l].reshape(db, w // LANES, LANES, hn), nb=db, t=t_s,
                       rows=tl["chunk"], w=w, hn=hn, pp=pp, out_dtype=F32)
    x1_s = _mm_norm_res([cv_s, rw_s.reshape(db, t_s * w)], [w_out_b[:c], w_out_b[c:]], g_mix_post, xs_bm, tm=db, tk=tk_e,
                        name="mix_out_sample", a_batch_major=(False, True), res_batch_major=True)
    q_s = _norm_matmul(x1_s, g_xa_pre, w_q_b, tm=db, tn=tn_a, out_dtype=F32, name="q_sample", out_batch_major=True)
    o_s = _attn_sample(q_s.reshape(rows_s, xd), cache_mem_k, cache_mem_v, db, t_s)
    x2_s = _mm_norm_res([o_s.reshape(db, t_s * xd)], [w_o_b], g_xa_post, x1_s, tm=db, tk=tk_e, name="attn_out_sample",
                        a_batch_major=(True,))
    ys2, ffn_s = _ffn_sample(x2_s, state_ffn[l].reshape(db, 4 * dff), g_ffn_pre, w_up_b, cw3, cb2, w_down_b, g_ffn_post,
                             db=db, t_s=t_s, tn=tn_f)
    shift_s = unpad_rcols(proj_s[(t_s - 1) * db:, 2 * c:])

    return (yp.reshape(nb, t, d), ys2.reshape(db, t_s, d),
            conv_p[None], conv_s.reshape(db, kc - 1, c)[None],
            shift_p[None], shift_s[None],
            wkv_p.reshape(nb, n_heads, hn, hn)[None], wkv_s.reshape(db, n_heads, hn, hn)[None],
            ffn_p[None], ffn_s[None],
            mk.reshape(nb, n_mem, xa_heads, xa_hd)[None], mv.reshape(nb, n_mem, xa_heads, xa_hd)[None])
```
